```python
import math
import jax, jax.numpy as jnp
from jax import lax
import numpy as np

D_MODEL = 2048
BATCH = 4
SEQ = 4096
DEPTH = 2

HEAD_DIM = 128
N_HEADS_TOTAL = D_MODEL // HEAD_DIM
N_FOX = N_HEADS_TOTAL // 2
N_DIFF = N_HEADS_TOTAL - N_FOX
FOX_WIDTH = N_FOX * HEAD_DIM
DIFF_WIDTH = N_DIFF * HEAD_DIM
MIX_WIDTH = FOX_WIDTH + DIFF_WIDTH
DIFF_QK_DIM = HEAD_DIM // 2
D_FF = 4 * D_MODEL
ROPE_THETA = 500000.0
ROT_DIM = DIFF_QK_DIM // 4
BLOCK_Q = 128
ALPHA = (2.0 * DEPTH) ** 0.25
BETA = (8.0 * DEPTH) ** -0.25
LN_EPS = 1e-5
RMS_EPS = 1e-5
ADA_SCALE = 0.5

_SIZES = [FOX_WIDTH, FOX_WIDTH, FOX_WIDTH, N_FOX, DIFF_WIDTH, DIFF_WIDTH, DIFF_WIDTH]
IN_COLS = sum(_SIZES)
_SPLITS = [int(v) for v in np.cumsum(_SIZES)[:-1]]

kernel_name = "hybrid_fox_diffattn_deepnorm_adaln"


def _layernorm(x, g, b):
    xf = x.astype(jnp.float32)
    mu = jnp.mean(xf, axis=-1, keepdims=True)
    var = jnp.mean(jnp.square(xf - mu), axis=-1, keepdims=True)
    y = (xf - mu) * lax.rsqrt(var + LN_EPS) * g.astype(jnp.float32) + b.astype(jnp.float32)
    return y.astype(x.dtype)


def _rope(x, cos, sin):
    half = ROT_DIM // 2
    c = cos[:, :, None, None, :]
    s = sin[:, :, None, None, :]
    x1 = x[..., :half]
    x2 = x[..., half:ROT_DIM]
    return jnp.concatenate([x1 * c - x2 * s, x2 * c + x1 * s, x[..., ROT_DIM:]], axis=-1)


def _lambda_init(layer_idx):
    return 0.8 - 0.6 * math.exp(-0.3 * layer_idx)


def _mixer(h, cos, sin, w_in, b_f, lq1, lk1, lq2, lk2, subln_g, w_o, lam_init):
    B, S, _ = h.shape
    f32 = jnp.float32
    proj = (h @ w_in).astype(f32)
    fq, fk, fv, ff, dq, dk, dv = jnp.split(proj, _SPLITS, axis=-1)
    fq = fq.reshape(B, S, N_FOX, HEAD_DIM)
    fk = fk.reshape(B, S, N_FOX, HEAD_DIM)
    fv = fv.reshape(B, S, N_FOX, HEAD_DIM)
    log_f = jax.nn.log_sigmoid(ff + b_f.astype(f32))
    fcum = jnp.cumsum(log_f, axis=1).transpose(0, 2, 1)
    dq = _rope(dq.reshape(B, S, N_DIFF, 2, DIFF_QK_DIM), cos, sin)
    dk = _rope(dk.reshape(B, S, N_DIFF, 2, DIFF_QK_DIM), cos, sin)
    dv = dv.reshape(B, S, N_DIFF, HEAD_DIM)
    lam = (jnp.exp(jnp.sum(lq1.astype(f32) * lk1.astype(f32)))
           - jnp.exp(jnp.sum(lq2.astype(f32) * lk2.astype(f32))) + lam_init)
    fox_scale = HEAD_DIM ** -0.5
    diff_scale = DIFF_QK_DIM ** -0.5
    key_pos = jnp.arange(S)

    def block(i):
        start = i * BLOCK_Q
        q_pos = start + jnp.arange(BLOCK_Q)
        causal = key_pos[None, :] <= q_pos[:, None]
        fq_b = lax.dynamic_slice_in_dim(fq, start, BLOCK_Q, axis=1)
        fc_b = lax.dynamic_slice_in_dim(fcum, start, BLOCK_Q, axis=2)
        s1 = (jnp.einsum('bqhd,bkhd->bhqk', fq_b, fk) * fox_scale
              + fc_b[:, :, :, None] - fcum[:, :, None, :])
        p1 = jax.nn.softmax(jnp.where(causal, s1, -jnp.inf), axis=-1)
        fox_o = jnp.einsum('bhqk,bkhd->bqhd', p1, fv)
        dq_b = lax.dynamic_slice_in_dim(dq, start, BLOCK_Q, axis=1)
        s2 = jnp.einsum('bqhcd,bkhcd->bhcqk', dq_b, dk) * diff_scale
        p2 = jax.nn.softmax(jnp.where(causal, s2, -jnp.inf), axis=-1)
        a = p2[:, :, 0] - lam * p2[:, :, 1]
        diff_o = jnp.einsum('bhqk,bkhd->bqhd', a, dv)
        return fox_o, diff_o

    fox_o, diff_o = lax.map(block, jnp.arange(S // BLOCK_Q))
    fox_o = jnp.moveaxis(fox_o, 0, 1).reshape(B, S, FOX_WIDTH)
    diff_o = jnp.moveaxis(diff_o, 0, 1).reshape(B, S, N_DIFF, HEAD_DIM)
    diff_o = diff_o * lax.rsqrt(jnp.mean(jnp.square(diff_o), axis=-1, keepdims=True) + RMS_EPS)
    diff_o = (diff_o * subln_g.astype(f32) * (1.0 - lam_init)).reshape(B, S, DIFF_WIDTH)
    mixed = jnp.concatenate([fox_o, diff_o], axis=-1).astype(h.dtype)
    return mixed @ w_o


def _mlp(h, w_up, w_down):
    return jnp.square(jax.nn.relu(h @ w_up)) @ w_down


def setup_inputs(seed: int = 0) -> dict:
    key = jax.random.key(seed)
    ks = jax.random.split(key, 20)
    f32 = jnp.float32
    x = jax.random.normal(ks[0], (BATCH, SEQ, D_MODEL), f32)
    c = jax.random.normal(ks[1], (BATCH, D_MODEL), f32)
    positions = jnp.broadcast_to(jnp.arange(SEQ, dtype=jnp.int32)[None, :], (BATCH, SEQ))
    w_ada = jax.random.normal(ks[2], (DEPTH, D_MODEL, 6 * D_MODEL), f32) * (ADA_SCALE * D_MODEL ** -0.5)
    b_ada = jax.random.normal(ks[3], (DEPTH, 6 * D_MODEL), f32) * 0.02
    col_scale = jnp.concatenate([
        jnp.ones((2 * FOX_WIDTH,), f32), jnp.full((FOX_WIDTH,), BETA, f32),
        jnp.ones((N_FOX + 2 * DIFF_WIDTH,), f32), jnp.full((DIFF_WIDTH,), BETA, f32)])
    w_in = jax.random.normal(ks[4], (DEPTH, D_MODEL, IN_COLS), f32) * (D_MODEL ** -0.5) * col_scale
    b_f = 2.0 + 0.5 * jax.random.normal(ks[5], (DEPTH, N_FOX), f32)
    lambda_q1 = jax.random.normal(ks[6], (DEPTH, DIFF_QK_DIM), f32) * 0.1
    lambda_k1 = jax.random.normal(ks[7], (DEPTH, DIFF_QK_DIM), f32) * 0.1
    lambda_q2 = jax.random.normal(ks[8], (DEPTH, DIFF_QK_DIM), f32) * 0.1
    lambda_k2 = jax.random.normal(ks[9], (DEPTH, DIFF_QK_DIM), f32) * 0.1
    subln_g = 1.0 + 0.02 * jax.random.normal(ks[10], (DEPTH, HEAD_DIM), f32)
    w_o = jax.random.normal(ks[11], (DEPTH, MIX_WIDTH, D_MODEL), f32) * (MIX_WIDTH ** -0.5) * BETA
    ln1_g = 1.0 + 0.02 * jax.random.normal(ks[12], (DEPTH, D_MODEL), f32)
    ln1_b = 0.02 * jax.random.normal(ks[13], (DEPTH, D_MODEL), f32)
    w_up = jax.random.normal(ks[14], (DEPTH, D_MODEL, D_FF), f32) * (D_MODEL ** -0.5)
    w_down = jax.random.normal(ks[15], (DEPTH, D_FF, D_MODEL), f32) * (D_FF ** -0.5) * BETA
    ln2_g = 1.0 + 0.02 * jax.random.normal(ks[16], (DEPTH, D_MODEL), f32)
    ln2_b = 0.02 * jax.random.normal(ks[17], (DEPTH, D_MODEL), f32)
    return {"x": x, "c": c, "positions": positions, "w_ada": w_ada, "b_ada": b_ada,
            "w_in": w_in, "b_f": b_f, "lambda_q1": lambda_q1, "lambda_k1": lambda_k1,
            "lambda_q2": lambda_q2, "lambda_k2": lambda_k2, "subln_g": subln_g,
            "w_o": w_o, "ln1_g": ln1_g, "ln1_b": ln1_b, "w_up": w_up, "w_down": w_down,
            "ln2_g": ln2_g, "ln2_b": ln2_b}


def reference(x, c, positions, w_ada, b_ada, w_in, b_f, lambda_q1, lambda_k1,
              lambda_q2, lambda_k2, subln_g, w_o, ln1_g, ln1_b, w_up, w_down,
              ln2_g, ln2_b):
    inv_freq = ROPE_THETA ** (-jnp.arange(0, ROT_DIM, 2, dtype=jnp.float32) / ROT_DIM)
    ang = positions.astype(jnp.float32)[:, :, None] * inv_freq
    cos, sin = jnp.cos(ang), jnp.sin(ang)
    c_act = jax.nn.silu(c)
    for l in range(DEPTH):
        mod = c_act @ w_ada[l] + b_ada[l]
        sh_a, sc_a, g_a, sh_m, sc_m, g_m = jnp.split(mod[:, None, :], 6, axis=-1)
        h = x * (1.0 + sc_a) + sh_a
        y = _mixer(h, cos, sin, w_in[l], b_f[l], lambda_q1[l], lambda_k1[l],
                   lambda_q2[l], lambda_k2[l], subln_g[l], w_o[l], _lambda_init(l))
        x = _layernorm(ALPHA * x + (1.0 + g_a) * y, ln1_g[l], ln1_b[l])
        h = x * (1.0 + sc_m) + sh_m
        y = _mlp(h, w_up[l], w_down[l])
        x = _layernorm(ALPHA * x + (1.0 + g_m) * y, ln2_g[l], ln2_b[l])
    return x
```

```python
import functools
import math

import jax
import jax.numpy as jnp
from jax import lax
from jax.experimental import pallas as pl
from jax.experimental.pallas import tpu as pltpu

HEAD_DIM = 128
DIFF_QK_DIM = HEAD_DIM // 2
ROT_DIM = DIFF_QK_DIM // 4
ROT_HALF = ROT_DIM // 2
ROPE_THETA = 500000.0
LN_EPS = 1e-5
RMS_EPS = 1e-5
LANES = 128
SUBLANES = 8
VMEM_LIMIT_BYTES = 56 * 1024 * 1024

F32 = jnp.float32
BF16 = jnp.bfloat16


def _params(*semantics):
    return pltpu.CompilerParams(dimension_semantics=semantics, vmem_limit_bytes=VMEM_LIMIT_BYTES)


def _lambda_init(layer_idx):
    return 0.8 - 0.6 * math.exp(-0.3 * layer_idx)


def _layernorm(z, g, b):
    mu = jnp.mean(z, axis=-1, keepdims=True)
    zc = z - mu
    var = jnp.mean(zc * zc, axis=-1, keepdims=True)
    return zc * lax.rsqrt(var + LN_EPS) * g + b


def _ada_kernel(c_ref, w_ref, b_ref, o_ref):
    c = c_ref[...]
    c_act = (c * jax.nn.sigmoid(c)).astype(BF16)
    o_ref[...] = jnp.dot(c_act, w_ref[...].astype(BF16), preferred_element_type=F32) + b_ref[...]


def _ada_mod(c_pad, w_ada, b_ada, tn):
    depth, d, n = w_ada.shape
    rows = c_pad.shape[0]
    return pl.pallas_call(
        _ada_kernel,
        grid=(depth, n // tn),
        in_specs=[
            pl.BlockSpec((rows, d), lambda l, j: (0, 0)),
            pl.BlockSpec((None, d, tn), lambda l, j: (l, 0, j)),
            pl.BlockSpec((None, 1, tn), lambda l, j: (l, 0, j)),
        ],
        out_specs=pl.BlockSpec((None, rows, tn), lambda l, j: (l, 0, j)),
        out_shape=jax.ShapeDtypeStruct((depth, rows, n), F32),
        compiler_params=_params("arbitrary", "arbitrary"),
        name="ada_mod",
    )(c_pad, w_ada, b_ada.reshape(depth, 1, n))


def _lam_kernel(q1_ref, k1_ref, q2_ref, k2_ref, init_ref, o_ref):
    s1 = jnp.sum(q1_ref[...] * k1_ref[...], axis=-1, keepdims=True)
    s2 = jnp.sum(q2_ref[...] * k2_ref[...], axis=-1, keepdims=True)
    lam = jnp.exp(s1) - jnp.exp(s2) + init_ref[...]
    o_ref[...] = jnp.broadcast_to(lam, o_ref.shape)


def _lam_values(lq1, lk1, lq2, lk2):
    depth = lq1.shape[0]
    init = jnp.asarray([[_lambda_init(l)] for l in range(depth)], F32)
    return pl.pallas_call(
        _lam_kernel,
        out_shape=jax.ShapeDtypeStruct((depth, LANES), F32),
        name="lam_values",
    )(lq1, lk1, lq2, lk2, init)


def _rope_table_kernel(pos_ref, invf_ref, c_ref, sa_ref, sb_ref):
    ang = pos_ref[...].astype(F32) * invf_ref[...]
    lane = lax.broadcasted_iota(jnp.int32, ang.shape, 1) % DIFF_QK_DIM
    sin = jnp.sin(ang)
    c_ref[...] = jnp.cos(ang)
    sa_ref[...] = jnp.where(lane < ROT_HALF, -sin, 0.0)
    sb_ref[...] = jnp.where(lane >= ROT_HALF, sin, 0.0)


def _rope_tables(positions, tm):
    m = positions.size
    inv_freq = ROPE_THETA ** (-jnp.arange(0, ROT_DIM, 2, dtype=F32) / ROT_DIM)
    lane = jnp.arange(LANES) % DIFF_QK_DIM
    invf_row = jnp.where(lane < ROT_DIM, inv_freq[lane % ROT_HALF], 0.0).astype(F32)[None, :]
    spec = pl.BlockSpec((tm, LANES), lambda i: (i, 0))
    shape = jax.ShapeDtypeStruct((m, LANES), F32)
    return pl.pallas_call(
        _rope_table_kernel,
        grid=(m // tm,),
        in_specs=[pl.BlockSpec((tm, 1), lambda i: (i, 0)), pl.BlockSpec((1, LANES), lambda i: (0, 0))],
        out_specs=[spec, spec, spec],
        out_shape=[shape, shape, shape],
        compiler_params=_params("arbitrary"),
        name="rope_tables",
    )(positions.reshape(m, 1), invf_row)


def _rope_apply(acc, c, sa, sb):
    heads = acc.shape[1] // LANES
    outs = []
    for h in range(heads):
        xs = acc[:, h * LANES:(h + 1) * LANES]
        up = pltpu.roll(xs, LANES - ROT_HALF, 1)
        dn = pltpu.roll(xs, ROT_HALF, 1)
        outs.append(xs * c + up * sa + dn * sb)
    return jnp.concatenate(outs, axis=1)


def _inproj_kernel(x_ref, sc_ref, sh_ref, w_ref, wf_ref, c_ref, sa_ref, sb_ref,
                   proj_ref, ff_ref, h_ref, *, fox_scale, diff_scale):
    j = pl.program_id(1)

    @pl.when(j == 0)
    def _():
        h = (x_ref[...] * (1.0 + sc_ref[...]) + sh_ref[...]).astype(BF16)
        h_ref[...] = h
        ff_ref[...] = jnp.dot(h, wf_ref[...], preferred_element_type=F32)

    acc = jnp.dot(h_ref[...], w_ref[...], preferred_element_type=F32)

    @pl.when(j == 0)
    def _():
        proj_ref[...] = (acc * fox_scale).astype(proj_ref.dtype)

    @pl.when((j == 1) | (j == 2) | (j == 5))
    def _():
        proj_ref[...] = acc.astype(proj_ref.dtype)

    @pl.when(j == 3)
    def _():
        r = _rope_apply(acc, c_ref[...], sa_ref[...], sb_ref[...])
        proj_ref[...] = (r * diff_scale).astype(proj_ref.dtype)

    @pl.when(j == 4)
    def _():
        r = _rope_apply(acc, c_ref[...], sa_ref[...], sb_ref[...])
        proj_ref[...] = r.astype(proj_ref.dtype)


def _in_proj(x2, mod, mod_base, w_main, w_f, tables, tm, rows_per_batch):
    m, d = x2.shape
    w = w_main.shape[1] // 6
    bpr = rows_per_batch // tm
    nb = mod.shape[0]

    def mod_spec(part):
        return pl.BlockSpec((None, 1, d), lambda i, j: (mod_base + part + i // bpr, 0, 0))

    batch = m // rows_per_batch
    tab_spec = pl.BlockSpec((tm, LANES), lambda i, j: (i, 0))
    kernel = functools.partial(_inproj_kernel, fox_scale=HEAD_DIM ** -0.5, diff_scale=DIFF_QK_DIM ** -0.5)
    del nb
    return pl.pallas_call(
        kernel,
        grid=(m // tm, 6),
        in_specs=[
            pl.BlockSpec((tm, d), lambda i, j: (i, 0)),
            mod_spec(1 * batch), mod_spec(0 * batch),
            pl.BlockSpec((d, w), lambda i, j: (0, j)),
            pl.BlockSpec((d, LANES), lambda i, j: (0, 0)),
            tab_spec, tab_spec, tab_spec,
        ],
        out_specs=[
            pl.BlockSpec((tm, w), lambda i, j: (i, j)),
            pl.BlockSpec((tm, LANES), lambda i, j: (i, 0)),
        ],
        out_shape=[jax.ShapeDtypeStruct((m, 6 * w), BF16), jax.ShapeDtypeStruct((m, LANES), F32)],
        scratch_shapes=[pltpu.VMEM((tm, d), BF16)],
        compiler_params=_params("arbitrary", "arbitrary"),
        name="in_proj",
    )(x2, mod, mod, w_main, w_f, *tables)


def _fgate_kernel(ff_ref, bf_ref, o_ref, *, chunk):
    s = ff_ref.shape[0]
    row = lax.broadcasted_iota(jnp.int32, (chunk, chunk), 0)
    col = lax.broadcasted_iota(jnp.int32, (chunk, chunk), 1)
    tri = (col <= row).astype(BF16)
    carry = jnp.zeros((1, LANES), F32)
    for ci in range(s // chunk):
        x = ff_ref[ci * chunk:(ci + 1) * chunk, :] + bf_ref[...]
        lf = jnp.minimum(x, 0.0) - jnp.log1p(jnp.exp(-jnp.abs(x)))
        hi = lf.astype(BF16)
        r1 = lf - hi.astype(F32)
        mid = r1.astype(BF16)
        lo = (r1 - mid.astype(F32)).astype(BF16)
        cs = (jnp.dot(tri, hi, preferred_element_type=F32)
              + jnp.dot(tri, mid, preferred_element_type=F32)
              + jnp.dot(tri, lo, preferred_element_type=F32)) + carry
        carry = cs[chunk - 1:chunk, :]
        o_ref[:, ci * chunk:(ci + 1) * chunk] = (-cs).T[:SUBLANES, :]


def _fgate(ff, b_f_row, batch, seq, chunk):
    return pl.pallas_call(
        functools.partial(_fgate_kernel, chunk=chunk),
        grid=(batch,),
        in_specs=[pl.BlockSpec((seq, LANES), lambda b: (b, 0)), pl.BlockSpec((1, LANES), lambda b: (0, 0))],
        out_specs=pl.BlockSpec((None, SUBLANES, seq), lambda b: (b, 0, 0)),
        out_shape=jax.ShapeDtypeStruct((batch, SUBLANES, seq), F32),
        compiler_params=_params("arbitrary"),
        name="fgate",
    )(ff, b_f_row)


def _softmax_step(s, m, l, acc, v):
    m_new = jnp.maximum(m, jnp.max(s, axis=1, keepdims=True))
    alpha = jnp.exp(m - m_new)
    p = jnp.exp(s - m_new)
    l_new = alpha * l + jnp.sum(p, axis=1, keepdims=True)
    acc_new = alpha * acc + jnp.dot(p.astype(v.dtype), v, preferred_element_type=F32)
    return m_new, l_new, acc_new


def _causal_mask(t):
    row = lax.broadcasted_iota(jnp.int32, (t, t), 0)
    col = lax.broadcasted_iota(jnp.int32, (t, t), 1)
    return col <= row


_NT = (((1,), (1,)), ((), ()))


def _fox_kernel(q_ref, k_ref, v_ref, b_ref, o_ref, *, t):
    i = pl.program_id(2)
    q = q_ref[...]

    def block(j, carry, diag):
        start = pl.multiple_of(j * t, t)
        k = k_ref[pl.ds(start, t), :]
        v = v_ref[pl.ds(start, t), :]
        s = lax.dot_general(q, k, _NT, preferred_element_type=F32) + b_ref[pl.ds(j, 1), :]
        if diag:
            s = jnp.where(_causal_mask(t), s, -jnp.inf)
        return _softmax_step(s, *carry, v)

    init = (jnp.full((t, 1), -jnp.inf, F32), jnp.zeros((t, 1), F32), jnp.zeros((t, HEAD_DIM), F32))
    carry = lax.fori_loop(0, i, lambda j, c: block(j, c, False), init)
    _, l, acc = block(i, carry, True)
    o_ref[...] = (acc / l).astype(o_ref.dtype)


def _diff_kernel(q_ref, k_ref, v_ref, lam_ref, g_ref, o_ref, *, t, out_scale):
    i = pl.program_id(2)
    q = q_ref[...]
    lane = lax.broadcasted_iota(jnp.int32, q.shape, 1)
    zero = jnp.zeros_like(q)
    qa = jnp.where(lane < DIFF_QK_DIM, q, zero)
    qb = jnp.where(lane >= DIFF_QK_DIM, q, zero)

    def block(j, carry, diag):
        ca, cb = carry
        start = pl.multiple_of(j * t, t)
        k = k_ref[pl.ds(start, t), :]
        v = v_ref[pl.ds(start, t), :]
        sa = lax.dot_general(qa, k, _NT, preferred_element_type=F32)
        sb = lax.dot_general(qb, k, _NT, preferred_element_type=F32)
        if diag:
            mask = _causal_mask(t)
            sa = jnp.where(mask, sa, -jnp.inf)
            sb = jnp.where(mask, sb, -jnp.inf)
        return _softmax_step(sa, *ca, v), _softmax_step(sb, *cb, v)

    one = (jnp.full((t, 1), -jnp.inf, F32), jnp.zeros((t, 1), F32), jnp.zeros((t, HEAD_DIM), F32))
    carry = lax.fori_loop(0, i, lambda j, c: block(j, c, False), (one, one))
    (_, la, acca), (_, lb, accb) = block(i, carry, True)
    o = acca / la - lam_ref[...] * (accb / lb)
    o = o * lax.rsqrt(jnp.mean(o * o, axis=-1, keepdims=True) + RMS_EPS)
    o_ref[...] = (o * g_ref[...] * out_scale).astype(o_ref.dtype)


def _fox_attention(proj, fneg, batch, seq, n_heads, t):
    nq = seq // t
    qspec = pl.BlockSpec((t, HEAD_DIM), lambda b, h, i: (b * nq + i, h))
    kspec = pl.BlockSpec((seq, HEAD_DIM), lambda b, h, i: (b, n_heads + h))
    vspec = pl.BlockSpec((seq, HEAD_DIM), lambda b, h, i: (b, 2 * n_heads + h))
    bspec = pl.BlockSpec((None, nq, t), lambda b, h, i: (b * SUBLANES + h, 0, 0))
    return pl.pallas_call(
        functools.partial(_fox_kernel, t=t),
        grid=(batch, n_heads, nq),
        in_specs=[qspec, kspec, vspec, bspec],
        out_specs=pl.BlockSpec((t, HEAD_DIM), lambda b, h, i: (b * nq + i, h)),
        out_shape=jax.ShapeDtypeStruct((batch * seq, n_heads * HEAD_DIM), BF16),
        compiler_params=_params("arbitrary", "arbitrary", "arbitrary"),
        name="fox_attention",
    )(proj, proj, proj, fneg.reshape(batch * SUBLANES, nq, t))


def _diff_attention(proj, lam_row, g_row, batch, seq, n_heads, t, out_scale):
    nq = seq // t
    qspec = pl.BlockSpec((t, HEAD_DIM), lambda b, h, i: (b * nq + i, 3 * n_heads + h))
    kspec = pl.BlockSpec((seq, HEAD_DIM), lambda b, h, i: (b, 4 * n_heads + h))
    vspec = pl.BlockSpec((seq, HEAD_DIM), lambda b, h, i: (b, 5 * n_heads + h))
    row = pl.BlockSpec((1, LANES), lambda b, h, i: (0, 0))
    return pl.pallas_call(
        functools.partial(_diff_kernel, t=t, out_scale=out_scale),
        grid=(batch, n_heads, nq),
        in_specs=[qspec, kspec, vspec, row, row],
        out_specs=pl.BlockSpec((t, HEAD_DIM), lambda b, h, i: (b * nq + i, h)),
        out_shape=jax.ShapeDtypeStruct((batch * seq, n_heads * HEAD_DIM), BF16),
        compiler_params=_params("arbitrary", "arbitrary", "arbitrary"),
        name="diff_attention",
    )(proj, proj, proj, lam_row, g_row)


def _outproj_kernel(fo_ref, do_ref, wa_ref, wb_ref, x_ref, gate_ref, lg_ref, lb_ref, o_ref, *, alpha):
    y = (jnp.dot(fo_ref[...], wa_ref[...], preferred_element_type=F32)
         + jnp.dot(do_ref[...], wb_ref[...], preferred_element_type=F32))
    z = alpha * x_ref[...] + (1.0 + gate_ref[...]) * y
    o_ref[...] = _layernorm(z, lg_ref[...], lb_ref[...])


def _out_proj(fox_o, diff_o, w_oa, w_ob, x2, mod, gate_base, ln_g, ln_b, tm, rows_per_batch, alpha):
    m, d = x2.shape
    w = fox_o.shape[1]
    bpr = rows_per_batch // tm
    row = pl.BlockSpec((1, d), lambda i: (0, 0))
    return pl.pallas_call(
        functools.partial(_outproj_kernel, alpha=alpha),
        grid=(m // tm,),
        in_specs=[
            pl.BlockSpec((tm, w), lambda i: (i, 0)),
            pl.BlockSpec((tm, w), lambda i: (i, 0)),
            pl.BlockSpec((w, d), lambda i: (0, 0)),
            pl.BlockSpec((w, d), lambda i: (0, 0)),
            pl.BlockSpec((tm, d), lambda i: (i, 0)),
            pl.BlockSpec((None, 1, d), lambda i: (gate_base + i // bpr, 0, 0)),
            row, row,
        ],
        out_specs=pl.BlockSpec((tm, d), lambda i: (i, 0)),
        out_shape=jax.ShapeDtypeStruct((m, d), F32),
        compiler_params=_params("arbitrary"),
        name="out_proj_ln",
    )(fox_o, diff_o, w_oa, w_ob, x2, mod, ln_g, ln_b)


def _mlp_kernel(x_ref, sc_ref, sh_ref, gate_ref, wu_ref, wd_ref, lg_ref, lb_ref, o_ref, h_ref, *, alpha):
    f = pl.program_id(1)

    @pl.when(f == 0)
    def _():
        h_ref[...] = (x_ref[...] * (1.0 + sc_ref[...]) + sh_ref[...]).astype(BF16)

    u = jnp.dot(h_ref[...], wu_ref[...], preferred_element_type=F32)
    a = jnp.square(jnp.maximum(u, 0.0)).astype(BF16)
    y = jnp.dot(a, wd_ref[...], preferred_element_type=F32)

    @pl.when(f == 0)
    def _():
        o_ref[...] = y

    @pl.when(f > 0)
    def _():
        o_ref[...] += y

    @pl.when(f == pl.num_programs(1) - 1)
    def _():
        z = alpha * x_ref[...] + (1.0 + gate_ref[...]) * o_ref[...]
        o_ref[...] = _layernorm(z, lg_ref[...], lb_ref[...])


def _mlp(x2, mod, mod_base, batch, w_up, w_down, ln_g, ln_b, tm, tf, rows_per_batch, alpha):
    m, d = x2.shape
    dff = w_up.shape[1]
    bpr = rows_per_batch // tm

    def mod_spec(part):
        return pl.BlockSpec((None, 1, d), lambda i, f: (mod_base + part * batch + i // bpr, 0, 0))

    row = pl.BlockSpec((1, d), lambda i, f: (0, 0))
    return pl.pallas_call(
        functools.partial(_mlp_kernel, alpha=alpha),
        grid=(m // tm, dff // tf),
        in_specs=[
            pl.BlockSpec((tm, d), lambda i, f: (i, 0)),
            mod_spec(4), mod_spec(3), mod_spec(5),
            pl.BlockSpec((d, tf), lambda i, f: (0, f)),
            pl.BlockSpec((tf, d), lambda i, f: (f, 0)),
            row, row,
        ],
        out_specs=pl.BlockSpec((tm, d), lambda i, f: (i, 0)),
        out_shape=jax.ShapeDtypeStruct((m, d), F32),
        scratch_shapes=[pltpu.VMEM((tm, d), BF16)],
        compiler_params=_params("arbitrary", "arbitrary"),
        name="mlp_ln",
    )(x2, mod, mod, mod, w_up, w_down, ln_g, ln_b)


def kernel(x, c, positions, w_ada, b_ada, w_in, b_f, lambda_q1, lambda_k1, lambda_q2, lambda_k2,
           subln_g, w_o, ln1_g, ln1_b, w_up, w_down, ln2_g, ln2_b):
    batch, seq, d = x.shape
    depth = w_ada.shape[0]
    m = batch * seq
    width = d // 2
    n_heads = width // HEAD_DIM
    assert n_heads <= SUBLANES and d % (2 * HEAD_DIM) == 0
    alpha = (2.0 * depth) ** 0.25

    t_attn = min(512, seq)
    tm_proj = min(1024, seq)
    tm_out = min(512, seq)
    tm_mlp = min(512, seq)
    tf_mlp = min(1024, w_up.shape[2])
    tn_ada = min(1024, 6 * d)
    cum_chunk = min(256, seq)

    c_pad = jnp.pad(c, ((0, SUBLANES - batch % SUBLANES if batch % SUBLANES else 0), (0, 0)))
    mod = _ada_mod(c_pad, w_ada, b_ada, tn_ada)[:, :batch]
    mod = mod.reshape(depth, batch, 6, d).transpose(0, 2, 1, 3).reshape(depth * 6 * batch, 1, d)

    lam = _lam_values(lambda_q1, lambda_k1, lambda_q2, lambda_k2)
    tables = _rope_tables(positions, tm_proj)

    x2 = x.reshape(m, d)
    for l in range(depth):
        base = l * 6 * batch
        wl = w_in[l]
        w_main = jnp.concatenate([wl[:, :3 * width], wl[:, 3 * width + n_heads:]], axis=1).astype(BF16)
        w_f = jnp.pad(wl[:, 3 * width:3 * width + n_heads], ((0, 0), (0, LANES - n_heads))).astype(BF16)
        b_f_row = jnp.pad(b_f[l], (0, LANES - n_heads))[None, :]

        proj, ff = _in_proj(x2, mod, base, w_main, w_f, tables, tm_proj, seq)
        fneg = _fgate(ff, b_f_row, batch, seq, cum_chunk)
        fox_o = _fox_attention(proj, fneg, batch, seq, n_heads, t_attn)
        diff_o = _diff_attention(proj, lam[l][None, :], subln_g[l][None, :], batch, seq, n_heads, t_attn,
                                 1.0 - _lambda_init(l))
        w_ol = w_o[l].astype(BF16)
        x2 = _out_proj(fox_o, diff_o, w_ol[:width], w_ol[width:], x2, mod, base + 2 * batch,
                       ln1_g[l][None, :], ln1_b[l][None, :], tm_out, seq, alpha)
        x2 = _mlp(x2, mod, base, batch, w_up[l].astype(BF16), w_down[l].astype(BF16),
                  ln2_g[l][None, :], ln2_b[l][None, :], tm_mlp, tf_mlp, seq, alpha)
    return x2.reshape(batch, seq, d)
```

```python
import functools
import math

import jax
import jax.numpy as jnp
from jax import lax
from jax.experimental import pallas as pl
from jax.experimental.pallas import tpu as pltpu

HEAD_DIM = 128
DIFF_QK_DIM = HEAD_DIM // 2
ROT_DIM = DIFF_QK_DIM // 4
ROT_HALF = ROT_DIM // 2
ROPE_THETA = 500000.0
LN_EPS = 1e-5
RMS_EPS = 1e-5
LANES = 128
SUBLANES = 8
BF16_SUBLANES = 16
QC = 256
VT_ROWS = HEAD_DIM + BF16_SUBLANES
LOG2E = math.log2(math.e)
VMEM_LIMIT_BYTES = 56 * 1024 * 1024

F32 = jnp.float32
BF16 = jnp.bfloat16


def _params(*semantics):
    return pltpu.CompilerParams(dimension_semantics=semantics, vmem_limit_bytes=VMEM_LIMIT_BYTES)


def _lambda_init(layer_idx):
    return 0.8 - 0.6 * math.exp(-0.3 * layer_idx)


def _layernorm(z, g, b):
    mu = jnp.mean(z, axis=-1, keepdims=True)
    zc = z - mu
    var = jnp.mean(zc * zc, axis=-1, keepdims=True)
    return zc * lax.rsqrt(var + LN_EPS) * g + b


def _ada_kernel(c_ref, w_ref, b_ref, o_ref):
    c = c_ref[...]
    c_act = (c * jax.nn.sigmoid(c)).astype(BF16)
    o_ref[...] = jnp.dot(c_act, w_ref[...].astype(BF16), preferred_element_type=F32) + b_ref[...]


def _ada_mod(c_pad, w_ada, b_ada, tn):
    depth, d, n = w_ada.shape
    rows = c_pad.shape[0]
    return pl.pallas_call(
        _ada_kernel,
        grid=(depth, n // tn),
        in_specs=[
            pl.BlockSpec((rows, d), lambda l, j: (0, 0)),
            pl.BlockSpec((None, d, tn), lambda l, j: (l, 0, j)),
            pl.BlockSpec((None, 1, tn), lambda l, j: (l, 0, j)),
        ],
        out_specs=pl.BlockSpec((None, rows, tn), lambda l, j: (l, 0, j)),
        out_shape=jax.ShapeDtypeStruct((depth, rows, n), F32),
        compiler_params=_params("arbitrary", "arbitrary"),
        name="ada_mod",
    )(c_pad, w_ada, b_ada.reshape(depth, 1, n))


def _lam_kernel(q1_ref, k1_ref, q2_ref, k2_ref, init_ref, o_ref):
    s1 = jnp.sum(q1_ref[...] * k1_ref[...], axis=-1, keepdims=True)
    s2 = jnp.sum(q2_ref[...] * k2_ref[...], axis=-1, keepdims=True)
    lam = jnp.exp(s1) - jnp.exp(s2) + init_ref[...]
    o_ref[...] = jnp.broadcast_to(lam, o_ref.shape)


def _lam_values(lq1, lk1, lq2, lk2):
    depth = lq1.shape[0]
    init = jnp.asarray([[_lambda_init(l)] for l in range(depth)], F32)
    return pl.pallas_call(
        _lam_kernel,
        out_shape=jax.ShapeDtypeStruct((depth, LANES), F32),
        name="lam_values",
    )(lq1, lk1, lq2, lk2, init)


def _rope_table_kernel(pos_ref, invf_ref, c_ref, sa_ref, sb_ref):
    ang = pos_ref[...].astype(F32) * invf_ref[...]
    lane = lax.broadcasted_iota(jnp.int32, ang.shape, 1) % DIFF_QK_DIM
    sin = jnp.sin(ang)
    c_ref[...] = jnp.cos(ang)
    sa_ref[...] = jnp.where(lane < ROT_HALF, -sin, 0.0)
    sb_ref[...] = jnp.where(lane >= ROT_HALF, sin, 0.0)


def _rope_tables(positions, tm):
    m = positions.size
    inv_freq = ROPE_THETA ** (-jnp.arange(0, ROT_DIM, 2, dtype=F32) / ROT_DIM)
    lane = jnp.arange(LANES) % DIFF_QK_DIM
    invf_row = jnp.where(lane < ROT_DIM, inv_freq[lane % ROT_HALF], 0.0).astype(F32)[None, :]
    spec = pl.BlockSpec((tm, LANES), lambda i: (i, 0))
    shape = jax.ShapeDtypeStruct((m, LANES), F32)
    return pl.pallas_call(
        _rope_table_kernel,
        grid=(m // tm,),
        in_specs=[pl.BlockSpec((tm, 1), lambda i: (i, 0)), pl.BlockSpec((1, LANES), lambda i: (0, 0))],
        out_specs=[spec, spec, spec],
        out_shape=[shape, shape, shape],
        compiler_params=_params("arbitrary"),
        name="rope_tables",
    )(positions.reshape(m, 1), invf_row)


def _rope_apply(acc, c, sa, sb):
    heads = acc.shape[1] // LANES
    outs = []
    for h in range(heads):
        xs = acc[:, h * LANES:(h + 1) * LANES]
        up = pltpu.roll(xs, LANES - ROT_HALF, 1)
        dn = pltpu.roll(xs, ROT_HALF, 1)
        outs.append(xs * c + up * sa + dn * sb)
    return jnp.concatenate(outs, axis=1)


def _inproj_kernel(x_ref, sc_ref, sh_ref, w_ref, wf_ref, c_ref, sa_ref, sb_ref,
                   proj_ref, ff_ref, h_ref, *, fox_scale, diff_scale):
    j = pl.program_id(1)

    @pl.when(j == 0)
    def _():
        h = (x_ref[...] * (1.0 + sc_ref[...]) + sh_ref[...]).astype(BF16)
        h_ref[...] = h
        ff_ref[...] = jnp.dot(h, wf_ref[...], preferred_element_type=F32)

    acc = jnp.dot(h_ref[...], w_ref[...], preferred_element_type=F32)

    @pl.when(j == 0)
    def _():
        proj_ref[...] = (acc * fox_scale).astype(proj_ref.dtype)

    @pl.when((j == 1) | (j == 2) | (j == 5))
    def _():
        proj_ref[...] = acc.astype(proj_ref.dtype)

    @pl.when(j == 3)
    def _():
        r = _rope_apply(acc, c_ref[...], sa_ref[...], sb_ref[...])
        proj_ref[...] = (r * diff_scale).astype(proj_ref.dtype)

    @pl.when(j == 4)
    def _():
        r = _rope_apply(acc, c_ref[...], sa_ref[...], sb_ref[...])
        proj_ref[...] = r.astype(proj_ref.dtype)


def _in_proj(x2, mod, mod_base, w_main, w_f, tables, tm, rows_per_batch):
    m, d = x2.shape
    w = w_main.shape[1] // 6
    bpr = rows_per_batch // tm
    batch = m // rows_per_batch

    def mod_spec(part):
        return pl.BlockSpec((None, 1, d), lambda i, j: (mod_base + part * batch + i // bpr, 0, 0))

    tab_spec = pl.BlockSpec((tm, LANES), lambda i, j: (i, 0))
    kernel = functools.partial(_inproj_kernel, fox_scale=HEAD_DIM ** -0.5 * LOG2E,
                               diff_scale=DIFF_QK_DIM ** -0.5 * LOG2E)
    return pl.pallas_call(
        kernel,
        grid=(m // tm, 6),
        in_specs=[
            pl.BlockSpec((tm, d), lambda i, j: (i, 0)),
            mod_spec(1), mod_spec(0),
            pl.BlockSpec((d, w), lambda i, j: (0, j)),
            pl.BlockSpec((d, LANES), lambda i, j: (0, 0)),
            tab_spec, tab_spec, tab_spec,
        ],
        out_specs=[
            pl.BlockSpec((tm, w), lambda i, j: (i, j)),
            pl.BlockSpec((tm, LANES), lambda i, j: (i, 0)),
        ],
        out_shape=[jax.ShapeDtypeStruct((m, 6 * w), BF16), jax.ShapeDtypeStruct((m, LANES), F32)],
        scratch_shapes=[pltpu.VMEM((tm, d), BF16)],
        compiler_params=_params("arbitrary", "arbitrary"),
        name="in_proj",
    )(x2, mod, mod, w_main, w_f, *tables)


def _fgate_kernel(ff_ref, bf_ref, o_ref, *, chunk):
    s = ff_ref.shape[0]
    row = lax.broadcasted_iota(jnp.int32, (chunk, chunk), 0)
    col = lax.broadcasted_iota(jnp.int32, (chunk, chunk), 1)
    tri = (col <= row).astype(BF16)
    carry = jnp.zeros((1, LANES), F32)
    for ci in range(s // chunk):
        x = ff_ref[ci * chunk:(ci + 1) * chunk, :] + bf_ref[...]
        lf = jnp.minimum(x, 0.0) - jnp.log1p(jnp.exp(-jnp.abs(x)))
        hi = lf.astype(BF16)
        r1 = lf - hi.astype(F32)
        mid = r1.astype(BF16)
        lo = (r1 - mid.astype(F32)).astype(BF16)
        cs = (jnp.dot(tri, hi, preferred_element_type=F32)
              + jnp.dot(tri, mid, preferred_element_type=F32)
              + jnp.dot(tri, lo, preferred_element_type=F32)) + carry
        carry = cs[chunk - 1:chunk, :]
        o_ref[ci * chunk:(ci + 1) * chunk, :] = -cs


def _fgate(ff, b_f_row, batch, seq, chunk):
    return pl.pallas_call(
        functools.partial(_fgate_kernel, chunk=chunk),
        grid=(batch,),
        in_specs=[pl.BlockSpec((seq, LANES), lambda b: (b, 0)), pl.BlockSpec((1, LANES), lambda b: (0, 0))],
        out_specs=pl.BlockSpec((seq, LANES), lambda b: (b, 0)),
        out_shape=jax.ShapeDtypeStruct((batch * seq, LANES), F32),
        compiler_params=_params("arbitrary"),
        name="fgate",
    )(ff, b_f_row)


def _fill_vt(v_ref, vt_ref, chunk):
    seq = v_ref.shape[0]
    for c in range(seq // chunk):
        blk = v_ref[c * chunk:(c + 1) * chunk, :].astype(F32)
        vt_ref[0:HEAD_DIM, c * chunk:(c + 1) * chunk] = blk.T.astype(BF16)
    row = lax.broadcasted_iota(jnp.int32, (VT_ROWS - HEAD_DIM, seq), 0)
    vt_ref[HEAD_DIM:VT_ROWS, :] = jnp.where(row == 0, 1.0, 0.0).astype(BF16)


def _chain_step(s, m, acc, vt_blk):
    m_new = jnp.maximum(m, jnp.max(s, axis=0, keepdims=True))
    m_safe = jnp.where(m_new == -jnp.inf, 0.0, m_new)
    alpha = jnp.exp2(m - m_safe)
    p = jnp.exp2(s - m_safe).astype(BF16)
    acc_new = alpha * acc + jnp.dot(vt_blk, p, preferred_element_type=F32)
    return m_new, acc_new


def _key_le_query(keys, queries, key0, query0):
    kk = lax.broadcasted_iota(jnp.int32, (keys, queries), 0) + key0
    qq = lax.broadcasted_iota(jnp.int32, (keys, queries), 1) + query0
    return kk <= qq


def _finish(acc):
    o_t = acc[0:HEAD_DIM, :] / acc[HEAD_DIM:HEAD_DIM + 1, :]
    return o_t.T


def _causal_pipeline(i, n_chains, late, masks_of, issue_scores, consume, init):
    everyone = tuple(range(n_chains))
    issue_scores(0, 0, everyone)

    def body(jj, state):
        issue_scores(1, 2 * jj + 1, everyone)
        state = consume(0, 2 * jj, everyone, state, None)
        issue_scores(0, 2 * jj + 2, everyone)
        return consume(1, 2 * jj + 1, everyone, state, None)

    state = lax.fori_loop(0, i, body, init)
    issue_scores(1, 2 * i + 1, late)
    state = consume(0, 2 * i, everyone, state, masks_of(0))
    return consume(1, 2 * i + 1, late, state, masks_of(1))


def _setup_vt_and_q(i, q_ref, v_ref, vt_ref):
    @pl.when(i == 0)
    def _():
        _fill_vt(v_ref, vt_ref, 512)

    return q_ref[...].astype(F32).T.astype(BF16)


def _make_consume(s_ref, vt_ref, tk):
    def consume(slot, block, chains, state, masks):
        start = pl.multiple_of(block * tk, tk)
        vt_blk = vt_ref[:, pl.ds(start, tk)]
        state = list(state)
        for c in chains:
            s = s_ref[slot, c]
            if masks is not None and masks[c] is not None:
                s = jnp.where(masks[c], s, -jnp.inf)
            state[c] = _chain_step(s, state[c][0], state[c][1], vt_blk)
        return tuple(state)
    return consume


def _init_state(n_chains):
    return tuple((jnp.full((1, QC), -jnp.inf, F32), jnp.zeros((VT_ROWS, QC), F32)) for _ in range(n_chains))


def _fox_kernel(q_ref, k_ref, v_ref, f_ref, o_ref, vt_ref, bcol_ref, s_ref, *, t, tk):
    h = pl.program_id(1)
    i = pl.program_id(2)
    seq = k_ref.shape[0]
    nq = t // QC
    assert t == 2 * tk and tk == 2 * QC

    @pl.when(i == 0)
    def _():
        lane = lax.broadcasted_iota(jnp.int32, (seq, LANES), 1)
        col = jnp.sum(jnp.where(lane == h, f_ref[...], 0.0), axis=1, keepdims=True)
        bcol_ref[...] = jnp.broadcast_to(col * LOG2E, (seq, LANES))

    q_t = _setup_vt_and_q(i, q_ref, v_ref, vt_ref)
    q_cs = [q_t[:, c * QC:(c + 1) * QC] for c in range(nq)]

    def issue_scores(slot, block, chains):
        start = pl.multiple_of(block * tk, tk)
        k = k_ref[pl.ds(start, tk), :]
        b = bcol_ref[pl.ds(start, tk), :]
        for c in chains:
            s = jnp.dot(k, q_cs[c], preferred_element_type=F32)
            for u in range(QC // LANES):
                s_ref[slot, c, :, u * LANES:(u + 1) * LANES] = s[:, u * LANES:(u + 1) * LANES] + b

    def masks_of(d):
        tri = [_key_le_query(tk, QC, 0, 0), _key_le_query(tk, QC, 0, QC)]
        return tri + [None, None] if d == 0 else [None, None] + tri

    state = _causal_pipeline(i, nq, (2, 3), masks_of, issue_scores, _make_consume(s_ref, vt_ref, tk),
                             _init_state(nq))
    for c in range(nq):
        o_ref[c * QC:(c + 1) * QC, :] = _finish(state[c][1]).astype(o_ref.dtype)


def _diff_kernel(q_ref, k_ref, v_ref, lam_ref, g_ref, o_ref, vt_ref, s_ref, *, t, tk, out_scale):
    i = pl.program_id(2)
    nq = t // QC
    assert t == 2 * tk and tk == 2 * QC
    q_t = _setup_vt_and_q(i, q_ref, v_ref, vt_ref)
    feat = lax.broadcasted_iota(jnp.int32, q_t.shape, 0)
    zero = jnp.zeros_like(q_t)
    q_maps = [jnp.where(feat < DIFF_QK_DIM, q_t, zero), jnp.where(feat >= DIFF_QK_DIM, q_t, zero)]
    q_cs = [q_maps[r][:, c * QC:(c + 1) * QC] for c in range(nq) for r in range(2)]

    def issue_scores(slot, block, chains):
        start = pl.multiple_of(block * tk, tk)
        k = k_ref[pl.ds(start, tk), :]
        for c in chains:
            s_ref[slot, c] = jnp.dot(k, q_cs[c], preferred_element_type=F32)

    def masks_of(d):
        tri = [_key_le_query(tk, QC, 0, 0)] * 2 + [_key_le_query(tk, QC, 0, QC)] * 2
        return tri + [None] * 4 if d == 0 else [None] * 4 + tri

    state = _causal_pipeline(i, 2 * nq, (4, 5, 6, 7), masks_of, issue_scores,
                             _make_consume(s_ref, vt_ref, tk), _init_state(2 * nq))
    for c in range(nq):
        o = _finish(state[2 * c][1]) - lam_ref[...] * _finish(state[2 * c + 1][1])
        o = o * lax.rsqrt(jnp.mean(o * o, axis=-1, keepdims=True) + RMS_EPS)
        o_ref[c * QC:(c + 1) * QC, :] = (o * g_ref[...] * out_scale).astype(o_ref.dtype)


def _fox_attention(proj, fcum, batch, seq, n_heads, t):
    nq = seq // t
    tk = t // 2
    qspec = pl.BlockSpec((t, HEAD_DIM), lambda b, h, i: (b * nq + i, h))
    kspec = pl.BlockSpec((seq, HEAD_DIM), lambda b, h, i: (b, n_heads + h))
    vspec = pl.BlockSpec((seq, HEAD_DIM), lambda b, h, i: (b, 2 * n_heads + h))
    fspec = pl.BlockSpec((seq, LANES), lambda b, h, i: (b, 0))
    return pl.pallas_call(
        functools.partial(_fox_kernel, t=t, tk=tk),
        grid=(batch, n_heads, nq),
        in_specs=[qspec, kspec, vspec, fspec],
        out_specs=pl.BlockSpec((t, HEAD_DIM), lambda b, h, i: (b * nq + i, h)),
        out_shape=jax.ShapeDtypeStruct((batch * seq, n_heads * HEAD_DIM), BF16),
        scratch_shapes=[pltpu.VMEM((VT_ROWS, seq), BF16), pltpu.VMEM((seq, LANES), F32),
                        pltpu.VMEM((2, t // QC, tk, QC), F32)],
        compiler_params=_params("arbitrary", "arbitrary", "arbitrary"),
        name="fox_attention",
    )(proj, proj, proj, fcum)


def _diff_attention(proj, lam_row, g_row, batch, seq, n_heads, t, out_scale):
    nq = seq // t
    tk = t // 2
    qspec = pl.BlockSpec((t, HEAD_DIM), lambda b, h, i: (b * nq + i, 3 * n_heads + h))
    kspec = pl.BlockSpec((seq, HEAD_DIM), lambda b, h, i: (b, 4 * n_heads + h))
    vspec = pl.BlockSpec((seq, HEAD_DIM), lambda b, h, i: (b, 5 * n_heads + h))
    row = pl.BlockSpec((1, LANES), lambda b, h, i: (0, 0))
    return pl.pallas_call(
        functools.partial(_diff_kernel, t=t, tk=tk, out_scale=out_scale),
        grid=(batch, n_heads, nq),
        in_specs=[qspec, kspec, vspec, row, row],
        out_specs=pl.BlockSpec((t, HEAD_DIM), lambda b, h, i: (b * nq + i, h)),
        out_shape=jax.ShapeDtypeStruct((batch * seq, n_heads * HEAD_DIM), BF16),
        scratch_shapes=[pltpu.VMEM((VT_ROWS, seq), BF16), pltpu.VMEM((2, 2 * t // QC, tk, QC), F32)],
        compiler_params=_params("arbitrary", "arbitrary", "arbitrary"),
        name="diff_attention",
    )(proj, proj, proj, lam_row, g_row)


def _outproj_kernel(fo_ref, do_ref, wa_ref, wb_ref, x_ref, gate_ref, lg_ref, lb_ref, o_ref, *, alpha):
    y = (jnp.dot(fo_ref[...], wa_ref[...], preferred_element_type=F32)
         + jnp.dot(do_ref[...], wb_ref[...], preferred_element_type=F32))
    z = alpha * x_ref[...] + (1.0 + gate_ref[...]) * y
    o_ref[...] = _layernorm(z, lg_ref[...], lb_ref[...])


def _out_proj(fox_o, diff_o, w_oa, w_ob, x2, mod, gate_base, ln_g, ln_b, tm, rows_per_batch, alpha):
    m, d = x2.shape
    w = fox_o.shape[1]
    bpr = rows_per_batch // tm
    row = pl.BlockSpec((1, d), lambda i: (0, 0))
    return pl.pallas_call(
        functools.partial(_outproj_kernel, alpha=alpha),
        grid=(m // tm,),
        in_specs=[
            pl.BlockSpec((tm, w), lambda i: (i, 0)),
            pl.BlockSpec((tm, w), lambda i: (i, 0)),
            pl.BlockSpec((w, d), lambda i: (0, 0)),
            pl.BlockSpec((w, d), lambda i: (0, 0)),
            pl.BlockSpec((tm, d), lambda i: (i, 0)),
            pl.BlockSpec((None, 1, d), lambda i: (gate_base + i // bpr, 0, 0)),
            row, row,
        ],
        out_specs=pl.BlockSpec((tm, d), lambda i: (i, 0)),
        out_shape=jax.ShapeDtypeStruct((m, d), F32),
        compiler_params=_params("arbitrary"),
        name="out_proj_ln",
    )(fox_o, diff_o, w_oa, w_ob, x2, mod, ln_g, ln_b)


def _mlp_kernel(x_ref, sc_ref, sh_ref, gate_ref, wu_ref, wd_ref, lg_ref, lb_ref, o_ref, h_ref, *, alpha):
    f = pl.program_id(1)

    @pl.when(f == 0)
    def _():
        h_ref[...] = (x_ref[...] * (1.0 + sc_ref[...]) + sh_ref[...]).astype(BF16)

    u = jnp.dot(h_ref[...], wu_ref[...], preferred_element_type=F32)
    a = jnp.square(jnp.maximum(u, 0.0)).astype(BF16)
    y = jnp.dot(a, wd_ref[...], preferred_element_type=F32)

    @pl.when(f == 0)
    def _():
        o_ref[...] = y

    @pl.when(f > 0)
    def _():
        o_ref[...] += y

    @pl.when(f == pl.num_programs(1) - 1)
    def _():
        z = alpha * x_ref[...] + (1.0 + gate_ref[...]) * o_ref[...]
        o_ref[...] = _layernorm(z, lg_ref[...], lb_ref[...])


def _mlp(x2, mod, mod_base, batch, w_up, w_down, ln_g, ln_b, tm, tf, rows_per_batch, alpha):
    m, d = x2.shape
    dff = w_up.shape[1]
    bpr = rows_per_batch // tm

    def mod_spec(part):
        return pl.BlockSpec((None, 1, d), lambda i, f: (mod_base + part * batch + i // bpr, 0, 0))

    row = pl.BlockSpec((1, d), lambda i, f: (0, 0))
    return pl.pallas_call(
        functools.partial(_mlp_kernel, alpha=alpha),
        grid=(m // tm, dff // tf),
        in_specs=[
            pl.BlockSpec((tm, d), lambda i, f: (i, 0)),
            mod_spec(4), mod_spec(3), mod_spec(5),
            pl.BlockSpec((d, tf), lambda i, f: (0, f)),
            pl.BlockSpec((tf, d), lambda i, f: (f, 0)),
            row, row,
        ],
        out_specs=pl.BlockSpec((tm, d), lambda i, f: (i, 0)),
        out_shape=jax.ShapeDtypeStruct((m, d), F32),
        scratch_shapes=[pltpu.VMEM((tm, d), BF16)],
        compiler_params=_params("arbitrary", "arbitrary"),
        name="mlp_ln",
    )(x2, mod, mod, mod, w_up, w_down, ln_g, ln_b)


def kernel(x, c, positions, w_ada, b_ada, w_in, b_f, lambda_q1, lambda_k1, lambda_q2, lambda_k2,
           subln_g, w_o, ln1_g, ln1_b, w_up, w_down, ln2_g, ln2_b):
    batch, seq, d = x.shape
    depth = w_ada.shape[0]
    m = batch * seq
    width = d // 2
    n_heads = width // HEAD_DIM
    assert n_heads <= LANES and d % (2 * HEAD_DIM) == 0
    alpha = (2.0 * depth) ** 0.25

    t_attn = 4 * QC
    assert seq % t_attn == 0
    tm_proj = min(1024, seq)
    tm_out = min(512, seq)
    tm_mlp = min(512, seq)
    tf_mlp = min(1024, w_up.shape[2])
    tn_ada = min(1024, 6 * d)
    cum_chunk = min(256, seq)

    c_pad = jnp.pad(c, ((0, -batch % SUBLANES), (0, 0)))
    mod = _ada_mod(c_pad, w_ada, b_ada, tn_ada)[:, :batch]
    mod = mod.reshape(depth, batch, 6, d).transpose(0, 2, 1, 3).reshape(depth * 6 * batch, 1, d)

    lam = _lam_values(lambda_q1, lambda_k1, lambda_q2, lambda_k2)
    tables = _rope_tables(positions, tm_proj)

    x2 = x.reshape(m, d)
    for l in range(depth):
        base = l * 6 * batch
        wl = w_in[l]
        w_main = jnp.concatenate([wl[:, :3 * width], wl[:, 3 * width + n_heads:]], axis=1).astype(BF16)
        w_f = jnp.pad(wl[:, 3 * width:3 * width + n_heads], ((0, 0), (0, LANES - n_heads))).astype(BF16)
        b_f_row = jnp.pad(b_f[l], (0, LANES - n_heads))[None, :]

        proj, ff = _in_proj(x2, mod, base, w_main, w_f, tables, tm_proj, seq)
        fcum = _fgate(ff, b_f_row, batch, seq, cum_chunk)
        fox_o = _fox_attention(proj, fcum, batch, seq, n_heads, t_attn)
        diff_o = _diff_attention(proj, lam[l][None, :], subln_g[l][None, :], batch, seq, n_heads, t_attn,
                                 1.0 - _lambda_init(l))
        w_ol = w_o[l].astype(BF16)
        x2 = _out_proj(fox_o, diff_o, w_ol[:width], w_ol[width:], x2, mod, base + 2 * batch,
                       ln1_g[l][None, :], ln1_b[l][None, :], tm_out, seq, alpha)
        x2 = _mlp(x2, mod, base, batch, w_up[l].astype(BF16), w_down[l].astype(BF16),
                  ln2_g[l][None, :], ln2_b[l][None, :], tm_mlp, tf_mlp, seq, alpha)
    return x2.reshape(batch, seq, d)
```

```python
import functools
import math

import jax
import jax.numpy as jnp
from jax import lax
from jax.experimental import pallas as pl
from jax.experimental.pallas import tpu as pltpu

HEAD_DIM = 128
DIFF_QK_DIM = HEAD_DIM // 2
ROT_DIM = DIFF_QK_DIM // 4
ROT_HALF = ROT_DIM // 2
ROPE_THETA = 500000.0
LN_EPS = 1e-5
RMS_EPS = 1e-5
LANES = 128
SUBLANES = 8
BF16_SUBLANES = 16
QC = 256
VT_ROWS = HEAD_DIM + BF16_SUBLANES
LOG2E = math.log2(math.e)
VMEM_LIMIT_BYTES = 56 * 1024 * 1024

F32 = jnp.float32
BF16 = jnp.bfloat16


def _params(*semantics):
    return pltpu.CompilerParams(dimension_semantics=semantics, vmem_limit_bytes=VMEM_LIMIT_BYTES)


def _lambda_init(layer_idx):
    return 0.8 - 0.6 * math.exp(-0.3 * layer_idx)


def _layernorm(z, g, b):
    mu = jnp.mean(z, axis=-1, keepdims=True)
    zc = z - mu
    var = jnp.mean(zc * zc, axis=-1, keepdims=True)
    return zc * lax.rsqrt(var + LN_EPS) * g + b


def _ada_kernel(c_ref, w_ref, b_ref, o_ref):
    c = c_ref[...]
    c_act = (c * jax.nn.sigmoid(c)).astype(BF16)
    o_ref[...] = jnp.dot(c_act, w_ref[...].astype(BF16), preferred_element_type=F32) + b_ref[...]


def _ada_mod(c_pad, w_ada, b_ada, tn):
    depth, d, n = w_ada.shape
    rows = c_pad.shape[0]
    return pl.pallas_call(
        _ada_kernel,
        grid=(depth, n // tn),
        in_specs=[
            pl.BlockSpec((rows, d), lambda l, j: (0, 0)),
            pl.BlockSpec((None, d, tn), lambda l, j: (l, 0, j)),
            pl.BlockSpec((None, 1, tn), lambda l, j: (l, 0, j)),
        ],
        out_specs=pl.BlockSpec((None, rows, tn), lambda l, j: (l, 0, j)),
        out_shape=jax.ShapeDtypeStruct((depth, rows, n), F32),
        compiler_params=_params("arbitrary", "arbitrary"),
        name="ada_mod",
    )(c_pad, w_ada, b_ada.reshape(depth, 1, n))


def _lam_kernel(q1_ref, k1_ref, q2_ref, k2_ref, init_ref, o_ref):
    s1 = jnp.sum(q1_ref[...] * k1_ref[...], axis=-1, keepdims=True)
    s2 = jnp.sum(q2_ref[...] * k2_ref[...], axis=-1, keepdims=True)
    lam = jnp.exp(s1) - jnp.exp(s2) + init_ref[...]
    o_ref[...] = jnp.broadcast_to(lam, o_ref.shape)


def _lam_values(lq1, lk1, lq2, lk2):
    depth = lq1.shape[0]
    init = jnp.asarray([[_lambda_init(l)] for l in range(depth)], F32)
    return pl.pallas_call(
        _lam_kernel,
        out_shape=jax.ShapeDtypeStruct((depth, LANES), F32),
        name="lam_values",
    )(lq1, lk1, lq2, lk2, init)


def _rope_table_kernel(pos_ref, invf_ref, c_ref, s_ref):
    ang = pos_ref[...].astype(F32) * invf_ref[...]
    lane = lax.broadcasted_iota(jnp.int32, ang.shape, 1)
    sin = jnp.sin(ang)
    c_ref[...] = jnp.cos(ang)
    s_ref[...] = jnp.where(lane < LANES // 2, -sin, sin)


def _rope_tables(positions, tm):
    m = positions.size
    inv_freq = ROPE_THETA ** (-jnp.arange(0, ROT_DIM, 2, dtype=F32) / ROT_DIM)
    lane = jnp.arange(LANES) % (LANES // 2)
    invf_row = jnp.where(lane < ROT_DIM, inv_freq[lane % ROT_HALF], 0.0).astype(F32)[None, :]
    spec = pl.BlockSpec((tm, LANES), lambda i: (i, 0))
    shape = jax.ShapeDtypeStruct((m, LANES), F32)
    return pl.pallas_call(
        _rope_table_kernel,
        grid=(m // tm,),
        in_specs=[pl.BlockSpec((tm, 1), lambda i: (i, 0)), pl.BlockSpec((1, LANES), lambda i: (0, 0))],
        out_specs=[spec, spec],
        out_shape=[shape, shape],
        compiler_params=_params("arbitrary"),
        name="rope_tables",
    )(positions.reshape(m, 1), invf_row)


def _diff_head_columns(w):
    *lead, n = w.shape
    wh = w.reshape(*lead, n // HEAD_DIM, HEAD_DIM)
    h2, d2 = ROT_HALF, DIFF_QK_DIM
    parts = [wh[..., 0:h2], wh[..., d2:d2 + h2], wh[..., ROT_DIM:d2],
             wh[..., h2:ROT_DIM], wh[..., d2 + h2:d2 + ROT_DIM], wh[..., d2 + ROT_DIM:]]
    return jnp.concatenate(parts, axis=-1).reshape(*lead, n)


def _first_map_features(idx):
    return (idx < ROT_HALF) | ((idx >= ROT_DIM) & (idx < DIFF_QK_DIM + ROT_HALF))


def _inproj_kernel(x_ref, sc_ref, sh_ref, w_ref, wf_ref, c_ref, s_ref, proj_ref, ff_ref, h_ref,
                   *, fox_scale, diff_scale):
    j = pl.program_id(1)

    @pl.when(j == 0)
    def _():
        h = (x_ref[...] * (1.0 + sc_ref[...]) + sh_ref[...]).astype(BF16)
        h_ref[...] = h
        ff_ref[...] = jnp.dot(h, wf_ref[...], preferred_element_type=F32)

    acc = jnp.dot(h_ref[...], w_ref[...], preferred_element_type=F32)
    rotary = (j == 3) | (j == 4)
    scale = jnp.where(j == 0, fox_scale, jnp.where(j == 3, diff_scale, 1.0)).astype(F32)
    c_eff = jnp.where(rotary, c_ref[...], 1.0) * scale
    s_eff = jnp.where(rotary, s_ref[...], 0.0) * scale
    for hd in range(acc.shape[1] // LANES):
        xs = acc[:, hd * LANES:(hd + 1) * LANES]
        partner = pltpu.roll(xs, LANES // 2, 1)
        proj_ref[:, hd * LANES:(hd + 1) * LANES] = (xs * c_eff + partner * s_eff).astype(proj_ref.dtype)


def _in_proj(x2, mod, mod_base, w_main, w_f, layer, tables, tm, rows_per_batch):
    m, d = x2.shape
    w = w_main.shape[2] // 6
    bpr = rows_per_batch // tm
    batch = m // rows_per_batch

    def mod_spec(part):
        return pl.BlockSpec((None, 1, d), lambda i, j: (mod_base + part * batch + i // bpr, 0, 0))

    tab_spec = pl.BlockSpec((tm, LANES), lambda i, j: (i, 0))
    kernel = functools.partial(_inproj_kernel, fox_scale=HEAD_DIM ** -0.5 * LOG2E,
                               diff_scale=DIFF_QK_DIM ** -0.5 * LOG2E)
    return pl.pallas_call(
        kernel,
        grid=(m // tm, 6),
        in_specs=[
            pl.BlockSpec((tm, d), lambda i, j: (i, 0)),
            mod_spec(1), mod_spec(0),
            pl.BlockSpec((None, d, w), lambda i, j: (layer, 0, j)),
            pl.BlockSpec((None, d, LANES), lambda i, j: (layer, 0, 0)),
            tab_spec, tab_spec,
        ],
        out_specs=[
            pl.BlockSpec((tm, w), lambda i, j: (i, j)),
            pl.BlockSpec((tm, LANES), lambda i, j: (i, 0)),
        ],
        out_shape=[jax.ShapeDtypeStruct((m, 6 * w), BF16), jax.ShapeDtypeStruct((m, LANES), F32)],
        scratch_shapes=[pltpu.VMEM((tm, d), BF16)],
        compiler_params=_params("arbitrary", "arbitrary"),
        name="in_proj",
    )(x2, mod, mod, w_main, w_f, *tables)


def _fgate_kernel(ff_ref, bf_ref, o_ref, *, chunk):
    s = ff_ref.shape[0]
    row = lax.broadcasted_iota(jnp.int32, (chunk, chunk), 0)
    col = lax.broadcasted_iota(jnp.int32, (chunk, chunk), 1)
    tri = (col <= row).astype(BF16)
    carry = jnp.zeros((1, LANES), F32)
    for ci in range(s // chunk):
        x = ff_ref[ci * chunk:(ci + 1) * chunk, :] + bf_ref[...]
        lf = jnp.minimum(x, 0.0) - jnp.log1p(jnp.exp(-jnp.abs(x)))
        hi = lf.astype(BF16)
        r1 = lf - hi.astype(F32)
        mid = r1.astype(BF16)
        lo = (r1 - mid.astype(F32)).astype(BF16)
        cs = (jnp.dot(tri, hi, preferred_element_type=F32)
              + jnp.dot(tri, mid, preferred_element_type=F32)
              + jnp.dot(tri, lo, preferred_element_type=F32)) + carry
        carry = cs[chunk - 1:chunk, :]
        o_ref[ci * chunk:(ci + 1) * chunk, :] = -cs


def _fgate(ff, b_f_row, batch, seq, chunk):
    return pl.pallas_call(
        functools.partial(_fgate_kernel, chunk=chunk),
        grid=(batch,),
        in_specs=[pl.BlockSpec((seq, LANES), lambda b: (b, 0)), pl.BlockSpec((1, LANES), lambda b: (0, 0))],
        out_specs=pl.BlockSpec((seq, LANES), lambda b: (b, 0)),
        out_shape=jax.ShapeDtypeStruct((batch * seq, LANES), F32),
        compiler_params=_params("arbitrary"),
        name="fgate",
    )(ff, b_f_row)


def _fill_vt(v_ref, vt_ref, chunk):
    seq = v_ref.shape[0]
    for c in range(seq // chunk):
        blk = v_ref[c * chunk:(c + 1) * chunk, :].astype(F32)
        vt_ref[0:HEAD_DIM, c * chunk:(c + 1) * chunk] = blk.T.astype(BF16)
    row = lax.broadcasted_iota(jnp.int32, (VT_ROWS - HEAD_DIM, seq), 0)
    vt_ref[HEAD_DIM:VT_ROWS, :] = jnp.where(row == 0, 1.0, 0.0).astype(BF16)


def _chain_step(s, m, acc, vt_blk):
    m_new = jnp.maximum(m, jnp.max(s, axis=0, keepdims=True))
    m_safe = jnp.where(m_new == -jnp.inf, 0.0, m_new)
    alpha = jnp.exp2(m - m_safe)
    p = jnp.exp2(s - m_safe).astype(BF16)
    acc_new = alpha * acc + jnp.dot(vt_blk, p, preferred_element_type=F32)
    return m_new, acc_new


def _key_le_query(keys, queries, key0, query0):
    kk = lax.broadcasted_iota(jnp.int32, (keys, queries), 0) + key0
    qq = lax.broadcasted_iota(jnp.int32, (keys, queries), 1) + query0
    return kk <= qq


def _finish(acc):
    o_t = acc[0:HEAD_DIM, :] / acc[HEAD_DIM:HEAD_DIM + 1, :]
    return o_t.T


def _causal_pipeline(i, n_chains, late, masks_of, issue_scores, consume, init):
    everyone = tuple(range(n_chains))
    issue_scores(0, 0, everyone)

    def body(jj, state):
        issue_scores(1, 2 * jj + 1, everyone)
        state = consume(0, 2 * jj, everyone, state, None)
        issue_scores(0, 2 * jj + 2, everyone)
        return consume(1, 2 * jj + 1, everyone, state, None)

    state = lax.fori_loop(0, i, body, init)
    issue_scores(1, 2 * i + 1, late)
    state = consume(0, 2 * i, everyone, state, masks_of(0))
    return consume(1, 2 * i + 1, late, state, masks_of(1))


def _setup_vt_and_q(i, q_ref, v_ref, vt_ref):
    @pl.when(i == 0)
    def _():
        _fill_vt(v_ref, vt_ref, 512)

    return q_ref[...].astype(F32).T.astype(BF16)


def _make_consume(s_ref, vt_ref, tk):
    def consume(slot, block, chains, state, masks):
        start = pl.multiple_of(block * tk, tk)
        vt_blk = vt_ref[:, pl.ds(start, tk)]
        state = list(state)
        for c in chains:
            s = s_ref[slot, c]
            if masks is not None and masks[c] is not None:
                s = jnp.where(masks[c], s, -jnp.inf)
            state[c] = _chain_step(s, state[c][0], state[c][1], vt_blk)
        return tuple(state)
    return consume


def _init_state(n_chains):
    return tuple((jnp.full((1, QC), -jnp.inf, F32), jnp.zeros((VT_ROWS, QC), F32)) for _ in range(n_chains))


def _fox_kernel(q_ref, k_ref, v_ref, f_ref, o_ref, vt_ref, bcol_ref, s_ref, *, t, tk):
    h = pl.program_id(1)
    i = pl.program_id(2)
    seq = k_ref.shape[0]
    nq = t // QC
    assert t == 2 * tk and tk == 2 * QC

    @pl.when(i == 0)
    def _():
        lane = lax.broadcasted_iota(jnp.int32, (seq, LANES), 1)
        col = jnp.sum(jnp.where(lane == h, f_ref[...], 0.0), axis=1, keepdims=True)
        bcol_ref[...] = jnp.broadcast_to(col * LOG2E, (seq, LANES))

    q_t = _setup_vt_and_q(i, q_ref, v_ref, vt_ref)
    q_cs = [q_t[:, c * QC:(c + 1) * QC] for c in range(nq)]

    def issue_scores(slot, block, chains):
        start = pl.multiple_of(block * tk, tk)
        k = k_ref[pl.ds(start, tk), :]
        b = bcol_ref[pl.ds(start, tk), :]
        for c in chains:
            s = jnp.dot(k, q_cs[c], preferred_element_type=F32)
            for u in range(QC // LANES):
                s_ref[slot, c, :, u * LANES:(u + 1) * LANES] = s[:, u * LANES:(u + 1) * LANES] + b

    def masks_of(d):
        tri = [_key_le_query(tk, QC, 0, 0), _key_le_query(tk, QC, 0, QC)]
        return tri + [None, None] if d == 0 else [None, None] + tri

    state = _causal_pipeline(i, nq, (2, 3), masks_of, issue_scores, _make_consume(s_ref, vt_ref, tk),
                             _init_state(nq))
    for c in range(nq):
        o_ref[c * QC:(c + 1) * QC, :] = _finish(state[c][1]).astype(o_ref.dtype)


def _diff_kernel(q_ref, k_ref, v_ref, lam_ref, g_ref, o_ref, vt_ref, s_ref, *, t, tk, out_scale):
    i = pl.program_id(2)
    nq = t // QC
    assert t == 2 * tk and tk == 2 * QC
    q_t = _setup_vt_and_q(i, q_ref, v_ref, vt_ref)
    first = _first_map_features(lax.broadcasted_iota(jnp.int32, q_t.shape, 0))
    zero = jnp.zeros_like(q_t)
    q_maps = [jnp.where(first, q_t, zero), jnp.where(first, zero, q_t)]
    q_cs = [q_maps[r][:, c * QC:(c + 1) * QC] for c in range(nq) for r in range(2)]

    def issue_scores(slot, block, chains):
        start = pl.multiple_of(block * tk, tk)
        k = k_ref[pl.ds(start, tk), :]
        for c in chains:
            s_ref[slot, c] = jnp.dot(k, q_cs[c], preferred_element_type=F32)

    def masks_of(d):
        tri = [_key_le_query(tk, QC, 0, 0)] * 2 + [_key_le_query(tk, QC, 0, QC)] * 2
        return tri + [None] * 4 if d == 0 else [None] * 4 + tri

    state = _causal_pipeline(i, 2 * nq, (4, 5, 6, 7), masks_of, issue_scores,
                             _make_consume(s_ref, vt_ref, tk), _init_state(2 * nq))
    for c in range(nq):
        o = _finish(state[2 * c][1]) - lam_ref[...] * _finish(state[2 * c + 1][1])
        o = o * lax.rsqrt(jnp.mean(o * o, axis=-1, keepdims=True) + RMS_EPS)
        o_ref[c * QC:(c + 1) * QC, :] = (o * g_ref[...] * out_scale).astype(o_ref.dtype)


def _fox_attention(proj, fcum, batch, seq, n_heads, t):
    nq = seq // t
    tk = t // 2
    qspec = pl.BlockSpec((t, HEAD_DIM), lambda b, h, i: (b * nq + i, h))
    kspec = pl.BlockSpec((seq, HEAD_DIM), lambda b, h, i: (b, n_heads + h))
    vspec = pl.BlockSpec((seq, HEAD_DIM), lambda b, h, i: (b, 2 * n_heads + h))
    fspec = pl.BlockSpec((seq, LANES), lambda b, h, i: (b, 0))
    return pl.pallas_call(
        functools.partial(_fox_kernel, t=t, tk=tk),
        grid=(batch, n_heads, nq),
        in_specs=[qspec, kspec, vspec, fspec],
        out_specs=pl.BlockSpec((t, HEAD_DIM), lambda b, h, i: (b * nq + i, h)),
        out_shape=jax.ShapeDtypeStruct((batch * seq, n_heads * HEAD_DIM), BF16),
        scratch_shapes=[pltpu.VMEM((VT_ROWS, seq), BF16), pltpu.VMEM((seq, LANES), F32),
                        pltpu.VMEM((2, t // QC, tk, QC), F32)],
        compiler_params=_params("arbitrary", "arbitrary", "arbitrary"),
        name="fox_attention",
    )(proj, proj, proj, fcum)


def _diff_attention(proj, lam_row, g_row, batch, seq, n_heads, t, out_scale):
    nq = seq // t
    tk = t // 2
    qspec = pl.BlockSpec((t, HEAD_DIM), lambda b, h, i: (b * nq + i, 3 * n_heads + h))
    kspec = pl.BlockSpec((seq, HEAD_DIM), lambda b, h, i: (b, 4 * n_heads + h))
    vspec = pl.BlockSpec((seq, HEAD_DIM), lambda b, h, i: (b, 5 * n_heads + h))
    row = pl.BlockSpec((1, LANES), lambda b, h, i: (0, 0))
    return pl.pallas_call(
        functools.partial(_diff_kernel, t=t, tk=tk, out_scale=out_scale),
        grid=(batch, n_heads, nq),
        in_specs=[qspec, kspec, vspec, row, row],
        out_specs=pl.BlockSpec((t, HEAD_DIM), lambda b, h, i: (b * nq + i, h)),
        out_shape=jax.ShapeDtypeStruct((batch * seq, n_heads * HEAD_DIM), BF16),
        scratch_shapes=[pltpu.VMEM((VT_ROWS, seq), BF16), pltpu.VMEM((2, 2 * t // QC, tk, QC), F32)],
        compiler_params=_params("arbitrary", "arbitrary", "arbitrary"),
        name="diff_attention",
    )(proj, proj, proj, lam_row, g_row)


def _outproj_kernel(fo_ref, do_ref, wa_ref, wb_ref, x_ref, gate_ref, lg_ref, lb_ref, o_ref, *, alpha):
    y = (jnp.dot(fo_ref[...], wa_ref[...], preferred_element_type=F32)
         + jnp.dot(do_ref[...], wb_ref[...], preferred_element_type=F32))
    z = alpha * x_ref[...] + (1.0 + gate_ref[...]) * y
    o_ref[...] = _layernorm(z, lg_ref[...], lb_ref[...])


def _out_proj(fox_o, diff_o, w_o, layer, x2, mod, gate_base, ln_g, ln_b, tm, rows_per_batch, alpha):
    m, d = x2.shape
    w = fox_o.shape[1]
    bpr = rows_per_batch // tm
    row = pl.BlockSpec((1, d), lambda i: (0, 0))
    return pl.pallas_call(
        functools.partial(_outproj_kernel, alpha=alpha),
        grid=(m // tm,),
        in_specs=[
            pl.BlockSpec((tm, w), lambda i: (i, 0)),
            pl.BlockSpec((tm, w), lambda i: (i, 0)),
            pl.BlockSpec((None, w, d), lambda i: (layer, 0, 0)),
            pl.BlockSpec((None, w, d), lambda i: (layer, 1, 0)),
            pl.BlockSpec((tm, d), lambda i: (i, 0)),
            pl.BlockSpec((None, 1, d), lambda i: (gate_base + i // bpr, 0, 0)),
            row, row,
        ],
        out_specs=pl.BlockSpec((tm, d), lambda i: (i, 0)),
        out_shape=jax.ShapeDtypeStruct((m, d), F32),
        compiler_params=_params("arbitrary"),
        name="out_proj_ln",
    )(fox_o, diff_o, w_o, w_o, x2, mod, ln_g, ln_b)


def _mlp_kernel(x_ref, sc_ref, sh_ref, gate_ref, wu_ref, wd_ref, lg_ref, lb_ref, o_ref, h_ref, *, alpha):
    f = pl.program_id(1)

    @pl.when(f == 0)
    def _():
        h_ref[...] = (x_ref[...] * (1.0 + sc_ref[...]) + sh_ref[...]).astype(BF16)
        o_ref[...] = jnp.zeros_like(o_ref)

    u = jnp.dot(h_ref[...], wu_ref[...], preferred_element_type=F32)
    a = jnp.square(jnp.maximum(u, 0.0)).astype(BF16)
    o_ref[...] += jnp.dot(a, wd_ref[...], preferred_element_type=F32)

    @pl.when(f == pl.num_programs(1) - 1)
    def _():
        z = alpha * x_ref[...] + (1.0 + gate_ref[...]) * o_ref[...]
        o_ref[...] = _layernorm(z, lg_ref[...], lb_ref[...])


def _mlp(x2, mod, mod_base, batch, w_up, w_down, layer, ln_g, ln_b, tm, tf, rows_per_batch, alpha):
    m, d = x2.shape
    dff = w_up.shape[2]
    bpr = rows_per_batch // tm

    def mod_spec(part):
        return pl.BlockSpec((None, 1, d), lambda i, f: (mod_base + part * batch + i // bpr, 0, 0))

    row = pl.BlockSpec((1, d), lambda i, f: (0, 0))
    return pl.pallas_call(
        functools.partial(_mlp_kernel, alpha=alpha),
        grid=(m // tm, dff // tf),
        in_specs=[
            pl.BlockSpec((tm, d), lambda i, f: (i, 0)),
            mod_spec(4), mod_spec(3), mod_spec(5),
            pl.BlockSpec((None, d, tf), lambda i, f: (layer, 0, f)),
            pl.BlockSpec((None, tf, d), lambda i, f: (layer, f, 0)),
            row, row,
        ],
        out_specs=pl.BlockSpec((tm, d), lambda i, f: (i, 0)),
        out_shape=jax.ShapeDtypeStruct((m, d), F32),
        scratch_shapes=[pltpu.VMEM((tm, d), BF16)],
        compiler_params=_params("arbitrary", "arbitrary"),
        name="mlp_ln",
    )(x2, mod, mod, mod, w_up, w_down, ln_g, ln_b)


def kernel(x, c, positions, w_ada, b_ada, w_in, b_f, lambda_q1, lambda_k1, lambda_q2, lambda_k2,
           subln_g, w_o, ln1_g, ln1_b, w_up, w_down, ln2_g, ln2_b):
    batch, seq, d = x.shape
    depth = w_ada.shape[0]
    m = batch * seq
    width = d // 2
    n_heads = width // HEAD_DIM
    assert n_heads <= LANES and d % (2 * HEAD_DIM) == 0
    alpha = (2.0 * depth) ** 0.25

    t_attn = 4 * QC
    assert seq % t_attn == 0
    tm_proj = min(1024, seq)
    tm_out = min(512, seq)
    tm_mlp = min(512, seq)
    tf_mlp = min(1024, w_up.shape[2])
    tn_ada = min(1024, 6 * d)
    cum_chunk = min(256, seq)

    c_pad = jnp.pad(c, ((0, -batch % SUBLANES), (0, 0)))
    mod = _ada_mod(c_pad, w_ada, b_ada, tn_ada)[:, :batch]
    mod = mod.reshape(depth, batch, 6, d).transpose(0, 2, 1, 3).reshape(depth * 6 * batch, 1, d)

    lam = _lam_values(lambda_q1, lambda_k1, lambda_q2, lambda_k2)
    tables = _rope_tables(positions, tm_proj)

    gate0 = 3 * width
    diff0 = gate0 + n_heads
    w_main = jnp.concatenate([w_in[:, :, :gate0], _diff_head_columns(w_in[:, :, diff0:diff0 + 2 * width]),
                              w_in[:, :, diff0 + 2 * width:]], axis=2).astype(BF16)
    w_f = jnp.pad(w_in[:, :, gate0:diff0], ((0, 0), (0, 0), (0, LANES - n_heads))).astype(BF16)
    w_o16, w_up16, w_down16 = w_o.astype(BF16), w_up.astype(BF16), w_down.astype(BF16)

    x2 = x.reshape(m, d)
    for l in range(depth):
        base = l * 6 * batch
        b_f_row = jnp.pad(b_f[l], (0, LANES - n_heads))[None, :]

        proj, ff = _in_proj(x2, mod, base, w_main, w_f, l, tables, tm_proj, seq)
        fcum = _fgate(ff, b_f_row, batch, seq, cum_chunk)
        fox_o = _fox_attention(proj, fcum, batch, seq, n_heads, t_attn)
        diff_o = _diff_attention(proj, lam[l][None, :], subln_g[l][None, :], batch, seq, n_heads, t_attn,
                                 1.0 - _lambda_init(l))
        x2 = _out_proj(fox_o, diff_o, w_o16, l, x2, mod, base + 2 * batch,
                       ln1_g[l][None, :], ln1_b[l][None, :], tm_out, seq, alpha)
        x2 = _mlp(x2, mod, base, batch, w_up16, w_down16, l,
                  ln2_g[l][None, :], ln2_b[l][None, :], tm_mlp, tf_mlp, seq, alpha)
    return x2.reshape(batch, seq, d)
```

```python
import functools
import math

import jax
import jax.numpy as jnp
from jax import lax
from jax.experimental import pallas as pl
from jax.experimental.pallas import tpu as pltpu

HEAD_DIM = 128
DIFF_QK_DIM = HEAD_DIM // 2
ROT_DIM = DIFF_QK_DIM // 4
ROT_HALF = ROT_DIM // 2
ROPE_THETA = 500000.0
LN_EPS = 1e-5
RMS_EPS = 1e-5
LANES = 128
SUBLANES = 8
BF16_SUBLANES = 16
QC = 256
VT_ROWS = HEAD_DIM + BF16_SUBLANES
LOG2E = math.log2(math.e)
VMEM_LIMIT_BYTES = 56 * 1024 * 1024

F32 = jnp.float32
BF16 = jnp.bfloat16


def _params(*semantics):
    return pltpu.CompilerParams(dimension_semantics=semantics, vmem_limit_bytes=VMEM_LIMIT_BYTES)


def _lambda_init(layer_idx):
    return 0.8 - 0.6 * math.exp(-0.3 * layer_idx)


def _layernorm(z, g, b):
    mu = jnp.mean(z, axis=-1, keepdims=True)
    zc = z - mu
    var = jnp.mean(zc * zc, axis=-1, keepdims=True)
    return zc * lax.rsqrt(var + LN_EPS) * g + b


def _ada_kernel(c_ref, w_ref, b_ref, o_ref):
    c = c_ref[...]
    c_act = (c * jax.nn.sigmoid(c)).astype(BF16)
    o_ref[...] = jnp.dot(c_act, w_ref[...].astype(BF16), preferred_element_type=F32) + b_ref[...]


def _ada_mod(c_pad, w_ada, b_ada, tn):
    depth, d, n = w_ada.shape
    rows = c_pad.shape[0]
    return pl.pallas_call(
        _ada_kernel,
        grid=(depth, n // tn),
        in_specs=[
            pl.BlockSpec((rows, d), lambda l, j: (0, 0)),
            pl.BlockSpec((None, d, tn), lambda l, j: (l, 0, j)),
            pl.BlockSpec((None, 1, tn), lambda l, j: (l, 0, j)),
        ],
        out_specs=pl.BlockSpec((None, rows, tn), lambda l, j: (l, 0, j)),
        out_shape=jax.ShapeDtypeStruct((depth, rows, n), F32),
        compiler_params=_params("arbitrary", "arbitrary"),
        name="ada_mod",
    )(c_pad, w_ada, b_ada.reshape(depth, 1, n))


def _lam_kernel(q1_ref, k1_ref, q2_ref, k2_ref, init_ref, o_ref):
    s1 = jnp.sum(q1_ref[...] * k1_ref[...], axis=-1, keepdims=True)
    s2 = jnp.sum(q2_ref[...] * k2_ref[...], axis=-1, keepdims=True)
    lam = jnp.exp(s1) - jnp.exp(s2) + init_ref[...]
    o_ref[...] = jnp.broadcast_to(lam, o_ref.shape)


def _lam_values(lq1, lk1, lq2, lk2):
    depth = lq1.shape[0]
    init = jnp.asarray([[_lambda_init(l)] for l in range(depth)], F32)
    return pl.pallas_call(
        _lam_kernel,
        out_shape=jax.ShapeDtypeStruct((depth, LANES), F32),
        name="lam_values",
    )(lq1, lk1, lq2, lk2, init)


def _rope_table_kernel(pos_ref, invf_ref, c_ref, s_ref):
    ang = pos_ref[...].astype(F32) * invf_ref[...]
    lane = lax.broadcasted_iota(jnp.int32, ang.shape, 1)
    sin = jnp.sin(ang)
    c_ref[...] = jnp.cos(ang)
    s_ref[...] = jnp.where(lane < LANES // 2, -sin, sin)


def _rope_tables(positions, tm):
    m = positions.size
    inv_freq = ROPE_THETA ** (-jnp.arange(0, ROT_DIM, 2, dtype=F32) / ROT_DIM)
    lane = jnp.arange(LANES) % (LANES // 2)
    invf_row = jnp.where(lane < ROT_DIM, inv_freq[lane % ROT_HALF], 0.0).astype(F32)[None, :]
    spec = pl.BlockSpec((tm, LANES), lambda i: (i, 0))
    shape = jax.ShapeDtypeStruct((m, LANES), F32)
    return pl.pallas_call(
        _rope_table_kernel,
        grid=(m // tm,),
        in_specs=[pl.BlockSpec((tm, 1), lambda i: (i, 0)), pl.BlockSpec((1, LANES), lambda i: (0, 0))],
        out_specs=[spec, spec],
        out_shape=[shape, shape],
        compiler_params=_params("arbitrary"),
        name="rope_tables",
    )(positions.reshape(m, 1), invf_row)


def _diff_head_columns(w):
    *lead, n = w.shape
    wh = w.reshape(*lead, n // HEAD_DIM, HEAD_DIM)
    h2, d2 = ROT_HALF, DIFF_QK_DIM
    parts = [wh[..., 0:h2], wh[..., d2:d2 + h2], wh[..., ROT_DIM:d2],
             wh[..., h2:ROT_DIM], wh[..., d2 + h2:d2 + ROT_DIM], wh[..., d2 + ROT_DIM:]]
    return jnp.concatenate(parts, axis=-1).reshape(*lead, n)


def _first_map_features(idx):
    return (idx < ROT_HALF) | ((idx >= ROT_DIM) & (idx < DIFF_QK_DIM + ROT_HALF))


def _inproj_kernel(x_ref, sc_ref, sh_ref, w_ref, wf_ref, c_ref, s_ref, proj_ref, ff_ref, h_ref,
                   *, fox_scale, diff_scale):
    j = pl.program_id(1)

    @pl.when(j == 0)
    def _():
        h = (x_ref[...] * (1.0 + sc_ref[...]) + sh_ref[...]).astype(BF16)
        h_ref[...] = h
        ff_ref[...] = jnp.dot(h, wf_ref[...], preferred_element_type=F32)

    acc = jnp.dot(h_ref[...], w_ref[...], preferred_element_type=F32)
    rotary = (j == 3) | (j == 4)
    scale = jnp.where(j == 0, fox_scale, jnp.where(j == 3, diff_scale, 1.0)).astype(F32)
    c_eff = jnp.where(rotary, c_ref[...], 1.0) * scale
    s_eff = jnp.where(rotary, s_ref[...], 0.0) * scale
    for hd in range(acc.shape[1] // LANES):
        xs = acc[:, hd * LANES:(hd + 1) * LANES]
        partner = pltpu.roll(xs, LANES // 2, 1)
        proj_ref[:, hd * LANES:(hd + 1) * LANES] = (xs * c_eff + partner * s_eff).astype(proj_ref.dtype)


def _in_proj(x2, mod, mod_base, w_main, w_f, layer, tables, tm, rows_per_batch):
    m, d = x2.shape
    w = w_main.shape[2] // 6
    bpr = rows_per_batch // tm
    batch = m // rows_per_batch

    def mod_spec(part):
        return pl.BlockSpec((None, 1, d), lambda i, j: (mod_base + part * batch + i // bpr, 0, 0))

    tab_spec = pl.BlockSpec((tm, LANES), lambda i, j: (i, 0))
    kernel = functools.partial(_inproj_kernel, fox_scale=HEAD_DIM ** -0.5 * LOG2E,
                               diff_scale=DIFF_QK_DIM ** -0.5 * LOG2E)
    return pl.pallas_call(
        kernel,
        grid=(m // tm, 6),
        in_specs=[
            pl.BlockSpec((tm, d), lambda i, j: (i, 0)),
            mod_spec(1), mod_spec(0),
            pl.BlockSpec((None, d, w), lambda i, j: (layer, 0, j)),
            pl.BlockSpec((None, d, LANES), lambda i, j: (layer, 0, 0)),
            tab_spec, tab_spec,
        ],
        out_specs=[
            pl.BlockSpec((tm, w), lambda i, j: (i, j)),
            pl.BlockSpec((tm, LANES), lambda i, j: (i, 0)),
        ],
        out_shape=[jax.ShapeDtypeStruct((m, 6 * w), BF16), jax.ShapeDtypeStruct((m, LANES), F32)],
        scratch_shapes=[pltpu.VMEM((tm, d), BF16)],
        compiler_params=_params("arbitrary", "arbitrary"),
        name="in_proj",
    )(x2, mod, mod, w_main, w_f, *tables)


def _fgate_kernel(ff_ref, bf_ref, o_ref, *, chunk):
    s = ff_ref.shape[0]
    row = lax.broadcasted_iota(jnp.int32, (chunk, chunk), 0)
    col = lax.broadcasted_iota(jnp.int32, (chunk, chunk), 1)
    tri = (col <= row).astype(BF16)
    carry = jnp.zeros((1, LANES), F32)
    for ci in range(s // chunk):
        x = ff_ref[ci * chunk:(ci + 1) * chunk, :] + bf_ref[...]
        lf = jnp.minimum(x, 0.0) - jnp.log1p(jnp.exp(-jnp.abs(x)))
        hi = lf.astype(BF16)
        r1 = lf - hi.astype(F32)
        mid = r1.astype(BF16)
        lo = (r1 - mid.astype(F32)).astype(BF16)
        cs = (jnp.dot(tri, hi, preferred_element_type=F32)
              + jnp.dot(tri, mid, preferred_element_type=F32)
              + jnp.dot(tri, lo, preferred_element_type=F32)) + carry
        carry = cs[chunk - 1:chunk, :]
        o_ref[ci * chunk:(ci + 1) * chunk, :] = -cs


def _fgate(ff, b_f_row, batch, seq, chunk):
    return pl.pallas_call(
        functools.partial(_fgate_kernel, chunk=chunk),
        grid=(batch,),
        in_specs=[pl.BlockSpec((seq, LANES), lambda b: (b, 0)), pl.BlockSpec((1, LANES), lambda b: (0, 0))],
        out_specs=pl.BlockSpec((seq, LANES), lambda b: (b, 0)),
        out_shape=jax.ShapeDtypeStruct((batch * seq, LANES), F32),
        compiler_params=_params("arbitrary"),
        name="fgate",
    )(ff, b_f_row)


def _fill_vt(v_ref, vt_ref, chunk):
    seq = v_ref.shape[0]
    for c in range(seq // chunk):
        blk = v_ref[c * chunk:(c + 1) * chunk, :].astype(F32)
        vt_ref[0:HEAD_DIM, c * chunk:(c + 1) * chunk] = blk.T.astype(BF16)
    row = lax.broadcasted_iota(jnp.int32, (VT_ROWS - HEAD_DIM, seq), 0)
    vt_ref[HEAD_DIM:VT_ROWS, :] = jnp.where(row == 0, 1.0, 0.0).astype(BF16)


def _chain_step(s, m, acc, vt_blk):
    m_new = jnp.maximum(m, jnp.max(s, axis=0, keepdims=True))
    m_safe = jnp.where(m_new == -jnp.inf, 0.0, m_new)
    alpha = jnp.exp2(m - m_safe)
    p = jnp.exp2(s - m_safe).astype(BF16)
    acc_new = alpha * acc + jnp.dot(vt_blk, p, preferred_element_type=F32)
    return m_new, acc_new


def _key_le_query(keys, queries, key0, query0):
    kk = lax.broadcasted_iota(jnp.int32, (keys, queries), 0) + key0
    qq = lax.broadcasted_iota(jnp.int32, (keys, queries), 1) + query0
    return kk <= qq


def _finish(acc):
    o_t = acc[0:HEAD_DIM, :] / acc[HEAD_DIM:HEAD_DIM + 1, :]
    return o_t.T


def _keys_seen(kb, c, tk):
    return min(tk, (c + 1) * QC - kb * tk)


def _causal_pipeline(n_kb, n_chains, per_kb, issue_scores, consume, init):
    state = init
    issue_scores(0, 0, tuple(range(n_chains)))
    for kb in range(n_kb):
        if kb + 1 < n_kb:
            issue_scores((kb + 1) % 2, kb + 1, tuple(range(per_kb * (kb + 1), n_chains)))
        state = consume(kb % 2, kb, tuple(range(per_kb * kb, n_chains)), state)
    return state


def _make_consume(s_ref, vt_ref, tk):
    per_kb = tk // QC

    def consume(slot, kb, chains, state):
        state = list(state)
        for c in chains:
            keys = _keys_seen(kb, c, tk)
            s = s_ref[slot, c, 0:keys, :]
            if c < per_kb * (kb + 1):
                s = jnp.where(_key_le_query(keys, QC, 0, keys - QC), s, -jnp.inf)
            state[c] = _chain_step(s, state[c][0], state[c][1], vt_ref[:, kb * tk:kb * tk + keys])
        return tuple(state)
    return consume


def _init_state(n_chains):
    return tuple((jnp.full((1, QC), -jnp.inf, F32), jnp.zeros((VT_ROWS, QC), F32)) for _ in range(n_chains))


def _fox_kernel(q_ref, k_ref, v_ref, f_ref, o_ref, vt_ref, bcol_ref, s_ref, *, tk):
    h = pl.program_id(1)
    seq = k_ref.shape[0]
    nq = seq // QC
    _fill_vt(v_ref, vt_ref, tk)
    lane = lax.broadcasted_iota(jnp.int32, (seq, LANES), 1)
    col = jnp.sum(jnp.where(lane == h, f_ref[...], 0.0), axis=1, keepdims=True)
    bcol_ref[...] = jnp.broadcast_to(col * LOG2E, (seq, LANES))
    q_cs = [q_ref[c * QC:(c + 1) * QC, :].astype(F32).T.astype(BF16) for c in range(nq)]

    def issue_scores(slot, kb, chains):
        for c in chains:
            keys = _keys_seen(kb, c, tk)
            b = bcol_ref[kb * tk:kb * tk + keys, :]
            s = jnp.dot(k_ref[kb * tk:kb * tk + keys, :], q_cs[c], preferred_element_type=F32)
            for u in range(QC // LANES):
                s_ref[slot, c, 0:keys, u * LANES:(u + 1) * LANES] = s[:, u * LANES:(u + 1) * LANES] + b

    state = _causal_pipeline(seq // tk, nq, tk // QC, issue_scores, _make_consume(s_ref, vt_ref, tk),
                             _init_state(nq))
    for c in range(nq):
        o_ref[c * QC:(c + 1) * QC, :] = _finish(state[c][1]).astype(o_ref.dtype)


def _diff_kernel(q_ref, k_ref, v_ref, lam_ref, g_ref, o_ref, vt_ref, s_ref, om_ref, *, tk, out_scale):
    seq = k_ref.shape[0]
    nq = seq // QC
    _fill_vt(v_ref, vt_ref, tk)
    q_ts = [q_ref[c * QC:(c + 1) * QC, :].astype(F32).T for c in range(nq)]
    first = _first_map_features(lax.broadcasted_iota(jnp.int32, (HEAD_DIM, QC), 0))
    consume = _make_consume(s_ref, vt_ref, tk)

    def one_map(r, carry):
        keep = first == (r == 0)
        q_cs = [jnp.where(keep, q, 0.0).astype(BF16) for q in q_ts]

        def issue_scores(slot, kb, chains):
            for c in chains:
                keys = _keys_seen(kb, c, tk)
                s_ref[slot, c, 0:keys, :] = jnp.dot(k_ref[kb * tk:kb * tk + keys, :], q_cs[c],
                                                    preferred_element_type=F32)

        state = _causal_pipeline(seq // tk, nq, tk // QC, issue_scores, consume, _init_state(nq))
        for c in range(nq):
            om_ref[r, c * QC:(c + 1) * QC, :] = _finish(state[c][1])
        return carry

    lax.fori_loop(0, 2, one_map, 0)
    o = om_ref[0] - lam_ref[...] * om_ref[1]
    o = o * lax.rsqrt(jnp.mean(o * o, axis=-1, keepdims=True) + RMS_EPS)
    o_ref[...] = (o * g_ref[...] * out_scale).astype(o_ref.dtype)


def _head_block(seq, col0):
    return pl.BlockSpec((seq, HEAD_DIM), lambda b, h: (b, col0 + h))


def _fox_attention(proj, fcum, batch, seq, n_heads, tk):
    return pl.pallas_call(
        functools.partial(_fox_kernel, tk=tk),
        grid=(batch, n_heads),
        in_specs=[_head_block(seq, 0), _head_block(seq, n_heads), _head_block(seq, 2 * n_heads),
                  pl.BlockSpec((seq, LANES), lambda b, h: (b, 0))],
        out_specs=_head_block(seq, 0),
        out_shape=jax.ShapeDtypeStruct((batch * seq, n_heads * HEAD_DIM), BF16),
        scratch_shapes=[pltpu.VMEM((VT_ROWS, seq), BF16), pltpu.VMEM((seq, LANES), F32),
                        pltpu.VMEM((2, seq // QC, tk, QC), F32)],
        compiler_params=_params("arbitrary", "arbitrary"),
        name="fox_attention",
    )(proj, proj, proj, fcum)


def _diff_attention(proj, lam_row, g_row, batch, seq, n_heads, tk, out_scale):
    row = pl.BlockSpec((1, LANES), lambda b, h: (0, 0))
    return pl.pallas_call(
        functools.partial(_diff_kernel, tk=tk, out_scale=out_scale),
        grid=(batch, n_heads),
        in_specs=[_head_block(seq, 3 * n_heads), _head_block(seq, 4 * n_heads), _head_block(seq, 5 * n_heads),
                  row, row],
        out_specs=_head_block(seq, 0),
        out_shape=jax.ShapeDtypeStruct((batch * seq, n_heads * HEAD_DIM), BF16),
        scratch_shapes=[pltpu.VMEM((VT_ROWS, seq), BF16), pltpu.VMEM((2, seq // QC, tk, QC), F32),
                        pltpu.VMEM((2, seq, HEAD_DIM), F32)],
        compiler_params=_params("arbitrary", "arbitrary"),
        name="diff_attention",
    )(proj, proj, proj, lam_row, g_row)


def _outproj_kernel(fo_ref, do_ref, wa_ref, wb_ref, x_ref, gate_ref, lg_ref, lb_ref, o_ref, *, alpha):
    y = (jnp.dot(fo_ref[...], wa_ref[...], preferred_element_type=F32)
         + jnp.dot(do_ref[...], wb_ref[...], preferred_element_type=F32))
    z = alpha * x_ref[...] + (1.0 + gate_ref[...]) * y
    o_ref[...] = _layernorm(z, lg_ref[...], lb_ref[...])


def _out_proj(fox_o, diff_o, w_o, layer, x2, mod, gate_base, ln_g, ln_b, tm, rows_per_batch, alpha):
    m, d = x2.shape
    w = fox_o.shape[1]
    bpr = rows_per_batch // tm
    row = pl.BlockSpec((1, d), lambda i: (0, 0))
    return pl.pallas_call(
        functools.partial(_outproj_kernel, alpha=alpha),
        grid=(m // tm,),
        in_specs=[
            pl.BlockSpec((tm, w), lambda i: (i, 0)),
            pl.BlockSpec((tm, w), lambda i: (i, 0)),
            pl.BlockSpec((None, w, d), lambda i: (layer, 0, 0)),
            pl.BlockSpec((None, w, d), lambda i: (layer, 1, 0)),
            pl.BlockSpec((tm, d), lambda i: (i, 0)),
            pl.BlockSpec((None, 1, d), lambda i: (gate_base + i // bpr, 0, 0)),
            row, row,
        ],
        out_specs=pl.BlockSpec((tm, d), lambda i: (i, 0)),
        out_shape=jax.ShapeDtypeStruct((m, d), F32),
        compiler_params=_params("arbitrary"),
        name="out_proj_ln",
    )(fox_o, diff_o, w_o, w_o, x2, mod, ln_g, ln_b)


def _mlp_kernel(x_ref, sc_ref, sh_ref, gate_ref, wu_ref, wd_ref, lg_ref, lb_ref, o_ref, h_ref, *, alpha):
    f = pl.program_id(1)

    @pl.when(f == 0)
    def _():
        h_ref[...] = (x_ref[...] * (1.0 + sc_ref[...]) + sh_ref[...]).astype(BF16)
        o_ref[...] = jnp.zeros_like(o_ref)

    u = jnp.dot(h_ref[...], wu_ref[...], preferred_element_type=F32)
    a = jnp.square(jnp.maximum(u, 0.0)).astype(BF16)
    o_ref[...] += jnp.dot(a, wd_ref[...], preferred_element_type=F32)

    @pl.when(f == pl.num_programs(1) - 1)
    def _():
        z = alpha * x_ref[...] + (1.0 + gate_ref[...]) * o_ref[...]
        o_ref[...] = _layernorm(z, lg_ref[...], lb_ref[...])


def _mlp(x2, mod, mod_base, batch, w_up, w_down, layer, ln_g, ln_b, tm, tf, rows_per_batch, alpha):
    m, d = x2.shape
    dff = w_up.shape[2]
    bpr = rows_per_batch // tm

    def mod_spec(part):
        return pl.BlockSpec((None, 1, d), lambda i, f: (mod_base + part * batch + i // bpr, 0, 0))

    row = pl.BlockSpec((1, d), lambda i, f: (0, 0))
    return pl.pallas_call(
        functools.partial(_mlp_kernel, alpha=alpha),
        grid=(m // tm, dff // tf),
        in_specs=[
            pl.BlockSpec((tm, d), lambda i, f: (i, 0)),
            mod_spec(4), mod_spec(3), mod_spec(5),
            pl.BlockSpec((None, d, tf), lambda i, f: (layer, 0, f)),
            pl.BlockSpec((None, tf, d), lambda i, f: (layer, f, 0)),
            row, row,
        ],
        out_specs=pl.BlockSpec((tm, d), lambda i, f: (i, 0)),
        out_shape=jax.ShapeDtypeStruct((m, d), F32),
        scratch_shapes=[pltpu.VMEM((tm, d), BF16)],
        compiler_params=_params("arbitrary", "arbitrary"),
        name="mlp_ln",
    )(x2, mod, mod, mod, w_up, w_down, ln_g, ln_b)


def kernel(x, c, positions, w_ada, b_ada, w_in, b_f, lambda_q1, lambda_k1, lambda_q2, lambda_k2,
           subln_g, w_o, ln1_g, ln1_b, w_up, w_down, ln2_g, ln2_b):
    batch, seq, d = x.shape
    depth = w_ada.shape[0]
    m = batch * seq
    width = d // 2
    n_heads = width // HEAD_DIM
    assert n_heads <= LANES and d % (2 * HEAD_DIM) == 0
    alpha = (2.0 * depth) ** 0.25

    tk_attn = 2 * QC
    assert seq % tk_attn == 0
    tm_proj = min(1024, seq)
    tm_out = min(512, seq)
    tm_mlp = min(512, seq)
    tf_mlp = min(1024, w_up.shape[2])
    tn_ada = min(1024, 6 * d)
    cum_chunk = min(256, seq)

    c_pad = jnp.pad(c, ((0, -batch % SUBLANES), (0, 0)))
    mod = _ada_mod(c_pad, w_ada, b_ada, tn_ada)[:, :batch]
    mod = mod.reshape(depth, batch, 6, d).transpose(0, 2, 1, 3).reshape(depth * 6 * batch, 1, d)

    lam = _lam_values(lambda_q1, lambda_k1, lambda_q2, lambda_k2)
    tables = _rope_tables(positions, tm_proj)

    gate0 = 3 * width
    diff0 = gate0 + n_heads
    w_main = jnp.concatenate([w_in[:, :, :gate0], _diff_head_columns(w_in[:, :, diff0:diff0 + 2 * width]),
                              w_in[:, :, diff0 + 2 * width:]], axis=2).astype(BF16)
    w_f = jnp.pad(w_in[:, :, gate0:diff0], ((0, 0), (0, 0), (0, LANES - n_heads))).astype(BF16)
    w_o16, w_up16, w_down16 = w_o.astype(BF16), w_up.astype(BF16), w_down.astype(BF16)

    x2 = x.reshape(m, d)
    for l in range(depth):
        base = l * 6 * batch
        b_f_row = jnp.pad(b_f[l], (0, LANES - n_heads))[None, :]

        proj, ff = _in_proj(x2, mod, base, w_main, w_f, l, tables, tm_proj, seq)
        fcum = _fgate(ff, b_f_row, batch, seq, cum_chunk)
        fox_o = _fox_attention(proj, fcum, batch, seq, n_heads, tk_attn)
        diff_o = _diff_attention(proj, lam[l][None, :], subln_g[l][None, :], batch, seq, n_heads, tk_attn,
                                 1.0 - _lambda_init(l))
        x2 = _out_proj(fox_o, diff_o, w_o16, l, x2, mod, base + 2 * batch,
                       ln1_g[l][None, :], ln1_b[l][None, :], tm_out, seq, alpha)
        x2 = _mlp(x2, mod, base, batch, w_up16, w_down16, l,
                  ln2_g[l][None, :], ln2_b[l][None, :], tm_mlp, tf_mlp, seq, alpha)
    return x2.reshape(batch, seq, d)
```

```python
import functools
import math

import jax
import jax.numpy as jnp
from jax import lax
from jax.experimental import pallas as pl
from jax.experimental.pallas import tpu as pltpu

HEAD_DIM = 128
DIFF_QK_DIM = HEAD_DIM // 2
ROT_DIM = DIFF_QK_DIM // 4
ROT_HALF = ROT_DIM // 2
ROPE_THETA = 500000.0
LN_EPS = 1e-5
RMS_EPS = 1e-5
LANES = 128
SUBLANES = 8
BF16_SUBLANES = 16
QC = 256
VT_ROWS = HEAD_DIM + BF16_SUBLANES
LOG2E = math.log2(math.e)
VMEM_LIMIT_BYTES = 56 * 1024 * 1024

F32 = jnp.float32
BF16 = jnp.bfloat16


def _params(*semantics):
    return pltpu.CompilerParams(dimension_semantics=semantics, vmem_limit_bytes=VMEM_LIMIT_BYTES)


def _lambda_init(layer_idx):
    return 0.8 - 0.6 * math.exp(-0.3 * layer_idx)


def _layernorm(z, g, b):
    mu = jnp.mean(z, axis=-1, keepdims=True)
    zc = z - mu
    var = jnp.mean(zc * zc, axis=-1, keepdims=True)
    return zc * lax.rsqrt(var + LN_EPS) * g + b


def _ada_kernel(c_ref, w_ref, b_ref, o_ref):
    c = c_ref[...]
    c_act = (c * jax.nn.sigmoid(c)).astype(BF16)
    o_ref[...] = jnp.dot(c_act, w_ref[...].astype(BF16), preferred_element_type=F32) + b_ref[...]


def _ada_mod(c_pad, w_ada, b_ada, tn):
    depth, d, n = w_ada.shape
    rows = c_pad.shape[0]
    return pl.pallas_call(
        _ada_kernel,
        grid=(depth, n // tn),
        in_specs=[
            pl.BlockSpec((rows, d), lambda l, j: (0, 0)),
            pl.BlockSpec((None, d, tn), lambda l, j: (l, 0, j)),
            pl.BlockSpec((None, 1, tn), lambda l, j: (l, 0, j)),
        ],
        out_specs=pl.BlockSpec((None, rows, tn), lambda l, j: (l, 0, j)),
        out_shape=jax.ShapeDtypeStruct((depth, rows, n), F32),
        compiler_params=_params("arbitrary", "arbitrary"),
        name="ada_mod",
    )(c_pad, w_ada, b_ada.reshape(depth, 1, n))


def _lam_kernel(q1_ref, k1_ref, q2_ref, k2_ref, init_ref, o_ref):
    s1 = jnp.sum(q1_ref[...] * k1_ref[...], axis=-1, keepdims=True)
    s2 = jnp.sum(q2_ref[...] * k2_ref[...], axis=-1, keepdims=True)
    lam = jnp.exp(s1) - jnp.exp(s2) + init_ref[...]
    o_ref[...] = jnp.broadcast_to(lam, o_ref.shape)


def _lam_values(lq1, lk1, lq2, lk2):
    depth = lq1.shape[0]
    init = jnp.asarray([[_lambda_init(l)] for l in range(depth)], F32)
    return pl.pallas_call(
        _lam_kernel,
        out_shape=jax.ShapeDtypeStruct((depth, LANES), F32),
        name="lam_values",
    )(lq1, lk1, lq2, lk2, init)


def _rope_table_kernel(pos_ref, invf_ref, c_ref, s_ref):
    ang = pos_ref[...].astype(F32) * invf_ref[...]
    lane = lax.broadcasted_iota(jnp.int32, ang.shape, 1)
    sin = jnp.sin(ang)
    c_ref[...] = jnp.cos(ang)
    s_ref[...] = jnp.where(lane < LANES // 2, -sin, sin)


def _rope_tables(positions, tm):
    m = positions.size
    inv_freq = ROPE_THETA ** (-jnp.arange(0, ROT_DIM, 2, dtype=F32) / ROT_DIM)
    lane = jnp.arange(LANES) % (LANES // 2)
    invf_row = jnp.where(lane < ROT_DIM, inv_freq[lane % ROT_HALF], 0.0).astype(F32)[None, :]
    spec = pl.BlockSpec((tm, LANES), lambda i: (i, 0))
    shape = jax.ShapeDtypeStruct((m, LANES), F32)
    return pl.pallas_call(
        _rope_table_kernel,
        grid=(m // tm,),
        in_specs=[pl.BlockSpec((tm, 1), lambda i: (i, 0)), pl.BlockSpec((1, LANES), lambda i: (0, 0))],
        out_specs=[spec, spec],
        out_shape=[shape, shape],
        compiler_params=_params("arbitrary"),
        name="rope_tables",
    )(positions.reshape(m, 1), invf_row)


def _diff_head_feature(lane):
    swap = DIFF_QK_DIM - ROT_HALF
    return jnp.where((lane >= ROT_HALF) & (lane < ROT_DIM), lane + swap,
                     jnp.where((lane >= DIFF_QK_DIM) & (lane < DIFF_QK_DIM + ROT_HALF), lane - swap, lane))


def _w_in_prep_kernel(w_ref, main_ref, f_ref, *, width, n_gate):
    gate0 = 3 * width
    diff0 = gate0 + n_gate
    main_ref[:, 0:gate0] = w_ref[:, 0:gate0].astype(BF16)
    lane = lax.broadcasted_iota(jnp.int32, (w_ref.shape[0], LANES), 1)
    f_ref[...] = jnp.where(lane < n_gate, w_ref[:, gate0:gate0 + LANES], 0.0).astype(BF16)
    src = lax.broadcasted_iota(jnp.int32, (HEAD_DIM, HEAD_DIM), 0)
    dst = lax.broadcasted_iota(jnp.int32, (HEAD_DIM, HEAD_DIM), 1)
    reorder = jnp.where(src == _diff_head_feature(dst), 1.0, 0.0).astype(BF16)
    heads = width // HEAD_DIM
    for hd in range(3 * heads):
        piece = w_ref[:, diff0 + hd * HEAD_DIM:diff0 + (hd + 1) * HEAD_DIM].astype(BF16)
        if hd < 2 * heads:
            piece = jnp.dot(piece, reorder, preferred_element_type=F32).astype(BF16)
        main_ref[:, gate0 + hd * HEAD_DIM:gate0 + (hd + 1) * HEAD_DIM] = piece


def _w_in_prep(w_in, width, n_gate, rows):
    depth, d, cols = w_in.shape
    return pl.pallas_call(
        functools.partial(_w_in_prep_kernel, width=width, n_gate=n_gate),
        grid=(depth, d // rows),
        in_specs=[pl.BlockSpec((None, rows, cols), lambda l, i: (l, i, 0))],
        out_specs=[pl.BlockSpec((None, rows, 6 * width), lambda l, i: (l, i, 0)),
                   pl.BlockSpec((None, rows, LANES), lambda l, i: (l, i, 0))],
        out_shape=[jax.ShapeDtypeStruct((depth, d, 6 * width), BF16), jax.ShapeDtypeStruct((depth, d, LANES), BF16)],
        compiler_params=_params("arbitrary", "arbitrary"),
        name="w_in_prep",
    )(w_in)


def _first_map_features(idx):
    return (idx < ROT_HALF) | ((idx >= ROT_DIM) & (idx < DIFF_QK_DIM + ROT_HALF))


def _inproj_kernel(x_ref, sc_ref, sh_ref, w_ref, wf_ref, c_ref, s_ref, proj_ref, ff_ref, h_ref,
                   *, fox_scale, diff_scale):
    j = pl.program_id(1)

    @pl.when(j == 0)
    def _():
        h = (x_ref[...] * (1.0 + sc_ref[...]) + sh_ref[...]).astype(BF16)
        h_ref[...] = h
        ff_ref[...] = jnp.dot(h, wf_ref[...], preferred_element_type=F32)

    acc = jnp.dot(h_ref[...], w_ref[...], preferred_element_type=F32)
    rotary = (j == 3) | (j == 4)
    scale = jnp.where(j == 0, fox_scale, jnp.where(j == 3, diff_scale, 1.0)).astype(F32)
    c_eff = jnp.where(rotary, c_ref[...], 1.0) * scale
    s_eff = jnp.where(rotary, s_ref[...], 0.0) * scale
    for hd in range(acc.shape[1] // LANES):
        xs = acc[:, hd * LANES:(hd + 1) * LANES]
        partner = pltpu.roll(xs, LANES // 2, 1)
        proj_ref[:, hd * LANES:(hd + 1) * LANES] = (xs * c_eff + partner * s_eff).astype(proj_ref.dtype)


def _in_proj(x2, mod, mod_base, w_main, w_f, layer, tables, tm, rows_per_batch):
    m, d = x2.shape
    w = w_main.shape[2] // 6
    bpr = rows_per_batch // tm
    batch = m // rows_per_batch

    def mod_spec(part):
        return pl.BlockSpec((None, 1, d), lambda i, j: (mod_base + part * batch + i // bpr, 0, 0))

    tab_spec = pl.BlockSpec((tm, LANES), lambda i, j: (i, 0))
    kernel = functools.partial(_inproj_kernel, fox_scale=HEAD_DIM ** -0.5 * LOG2E,
                               diff_scale=DIFF_QK_DIM ** -0.5 * LOG2E)
    return pl.pallas_call(
        kernel,
        grid=(m // tm, 6),
        in_specs=[
            pl.BlockSpec((tm, d), lambda i, j: (i, 0)),
            mod_spec(1), mod_spec(0),
            pl.BlockSpec((None, d, w), lambda i, j: (layer, 0, j)),
            pl.BlockSpec((None, d, LANES), lambda i, j: (layer, 0, 0)),
            tab_spec, tab_spec,
        ],
        out_specs=[
            pl.BlockSpec((tm, w), lambda i, j: (i, j)),
            pl.BlockSpec((tm, LANES), lambda i, j: (i, 0)),
        ],
        out_shape=[jax.ShapeDtypeStruct((m, 6 * w), BF16), jax.ShapeDtypeStruct((m, LANES), F32)],
        scratch_shapes=[pltpu.VMEM((tm, d), BF16)],
        compiler_params=_params("arbitrary", "arbitrary"),
        name="in_proj",
    )(x2, mod, mod, w_main, w_f, *tables)


def _fgate_kernel(ff_ref, bf_ref, o_ref, *, chunk):
    s = ff_ref.shape[0]
    row = lax.broadcasted_iota(jnp.int32, (chunk, chunk), 0)
    col = lax.broadcasted_iota(jnp.int32, (chunk, chunk), 1)
    tri = (col <= row).astype(BF16)
    carry = jnp.zeros((1, LANES), F32)
    for ci in range(s // chunk):
        x = ff_ref[ci * chunk:(ci + 1) * chunk, :] + bf_ref[...]
        lf = jnp.minimum(x, 0.0) - jnp.log1p(jnp.exp(-jnp.abs(x)))
        hi = lf.astype(BF16)
        r1 = lf - hi.astype(F32)
        mid = r1.astype(BF16)
        lo = (r1 - mid.astype(F32)).astype(BF16)
        cs = (jnp.dot(tri, hi, preferred_element_type=F32)
              + jnp.dot(tri, mid, preferred_element_type=F32)
              + jnp.dot(tri, lo, preferred_element_type=F32)) + carry
        carry = cs[chunk - 1:chunk, :]
        o_ref[ci * chunk:(ci + 1) * chunk, :] = -cs


def _fgate(ff, b_f_row, batch, seq, chunk):
    return pl.pallas_call(
        functools.partial(_fgate_kernel, chunk=chunk),
        grid=(batch,),
        in_specs=[pl.BlockSpec((seq, LANES), lambda b: (b, 0)), pl.BlockSpec((1, LANES), lambda b: (0, 0))],
        out_specs=pl.BlockSpec((seq, LANES), lambda b: (b, 0)),
        out_shape=jax.ShapeDtypeStruct((batch * seq, LANES), F32),
        compiler_params=_params("arbitrary"),
        name="fgate",
    )(ff, b_f_row)


def _fill_vt(v_ref, vt_ref, chunk):
    seq = v_ref.shape[0]
    for c in range(seq // chunk):
        blk = v_ref[c * chunk:(c + 1) * chunk, :].astype(F32)
        vt_ref[0:HEAD_DIM, c * chunk:(c + 1) * chunk] = blk.T.astype(BF16)
    row = lax.broadcasted_iota(jnp.int32, (VT_ROWS - HEAD_DIM, seq), 0)
    vt_ref[HEAD_DIM:VT_ROWS, :] = jnp.where(row == 0, 1.0, 0.0).astype(BF16)


def _chain_step(s, m, acc, vt_blk):
    m_new = jnp.maximum(m, jnp.max(s, axis=0, keepdims=True))
    m_safe = jnp.where(m_new == -jnp.inf, 0.0, m_new)
    alpha = jnp.exp2(m - m_safe)
    p = jnp.exp2(s - m_safe).astype(BF16)
    acc_new = alpha * acc + jnp.dot(vt_blk, p, preferred_element_type=F32)
    return m_new, acc_new


def _key_le_query(keys, queries, key0, query0):
    kk = lax.broadcasted_iota(jnp.int32, (keys, queries), 0) + key0
    qq = lax.broadcasted_iota(jnp.int32, (keys, queries), 1) + query0
    return kk <= qq


def _finish(acc):
    o_t = acc[0:HEAD_DIM, :] / acc[HEAD_DIM:HEAD_DIM + 1, :]
    return o_t.T


def _keys_seen(kb, c, tk):
    return min(tk, (c + 1) * QC - kb * tk)


def _causal_pipeline(n_kb, n_chains, per_kb, issue_scores, consume, init):
    state = init
    issue_scores(0, 0, tuple(range(n_chains)))
    for kb in range(n_kb):
        if kb + 1 < n_kb:
            issue_scores((kb + 1) % 2, kb + 1, tuple(range(per_kb * (kb + 1), n_chains)))
        state = consume(kb % 2, kb, tuple(range(per_kb * kb, n_chains)), state)
    return state


def _make_consume(s_ref, vt_ref, tk):
    per_kb = tk // QC

    def consume(slot, kb, chains, state):
        state = list(state)
        for c in chains:
            keys = _keys_seen(kb, c, tk)
            s = s_ref[slot, c, 0:keys, :]
            if c < per_kb * (kb + 1):
                s = jnp.where(_key_le_query(keys, QC, 0, keys - QC), s, -jnp.inf)
            state[c] = _chain_step(s, state[c][0], state[c][1], vt_ref[:, kb * tk:kb * tk + keys])
        return tuple(state)
    return consume


def _init_state(n_chains):
    return tuple((jnp.full((1, QC), -jnp.inf, F32), jnp.zeros((VT_ROWS, QC), F32)) for _ in range(n_chains))


def _fox_kernel(q_ref, k_ref, v_ref, f_ref, o_ref, vt_ref, bcol_ref, s_ref, *, tk):
    h = pl.program_id(1)
    seq = k_ref.shape[0]
    nq = seq // QC
    _fill_vt(v_ref, vt_ref, tk)
    lane = lax.broadcasted_iota(jnp.int32, (seq, LANES), 1)
    col = jnp.sum(jnp.where(lane == h, f_ref[...], 0.0), axis=1, keepdims=True)
    bcol_ref[...] = jnp.broadcast_to(col * LOG2E, (seq, LANES))
    q_cs = [q_ref[c * QC:(c + 1) * QC, :].astype(F32).T.astype(BF16) for c in range(nq)]

    def issue_scores(slot, kb, chains):
        for c in chains:
            keys = _keys_seen(kb, c, tk)
            b = bcol_ref[kb * tk:kb * tk + keys, :]
            s = jnp.dot(k_ref[kb * tk:kb * tk + keys, :], q_cs[c], preferred_element_type=F32)
            for u in range(QC // LANES):
                s_ref[slot, c, 0:keys, u * LANES:(u + 1) * LANES] = s[:, u * LANES:(u + 1) * LANES] + b

    state = _causal_pipeline(seq // tk, nq, tk // QC, issue_scores, _make_consume(s_ref, vt_ref, tk),
                             _init_state(nq))
    for c in range(nq):
        o_ref[c * QC:(c + 1) * QC, :] = _finish(state[c][1]).astype(o_ref.dtype)


def _diff_kernel(q_ref, k_ref, v_ref, lam_ref, g_ref, o_ref, vt_ref, s_ref, om_ref, *, tk, out_scale):
    seq = k_ref.shape[0]
    nq = seq // QC
    _fill_vt(v_ref, vt_ref, tk)
    q_ts = [q_ref[c * QC:(c + 1) * QC, :].astype(F32).T for c in range(nq)]
    first = _first_map_features(lax.broadcasted_iota(jnp.int32, (HEAD_DIM, QC), 0))
    consume = _make_consume(s_ref, vt_ref, tk)

    def one_map(r, carry):
        keep = first == (r == 0)
        q_cs = [jnp.where(keep, q, 0.0).astype(BF16) for q in q_ts]

        def issue_scores(slot, kb, chains):
            for c in chains:
                keys = _keys_seen(kb, c, tk)
                s_ref[slot, c, 0:keys, :] = jnp.dot(k_ref[kb * tk:kb * tk + keys, :], q_cs[c],
                                                    preferred_element_type=F32)

        state = _causal_pipeline(seq // tk, nq, tk // QC, issue_scores, consume, _init_state(nq))
        for c in range(nq):
            om_ref[r, c * QC:(c + 1) * QC, :] = _finish(state[c][1])
        return carry

    lax.fori_loop(0, 2, one_map, 0)
    o = om_ref[0] - lam_ref[...] * om_ref[1]
    o = o * lax.rsqrt(jnp.mean(o * o, axis=-1, keepdims=True) + RMS_EPS)
    o_ref[...] = (o * g_ref[...] * out_scale).astype(o_ref.dtype)


def _head_block(seq, col0):
    return pl.BlockSpec((seq, HEAD_DIM), lambda b, h: (b, col0 + h))


def _fox_attention(proj, fcum, batch, seq, n_heads, tk):
    return pl.pallas_call(
        functools.partial(_fox_kernel, tk=tk),
        grid=(batch, n_heads),
        in_specs=[_head_block(seq, 0), _head_block(seq, n_heads), _head_block(seq, 2 * n_heads),
                  pl.BlockSpec((seq, LANES), lambda b, h: (b, 0))],
        out_specs=_head_block(seq, 0),
        out_shape=jax.ShapeDtypeStruct((batch * seq, n_heads * HEAD_DIM), BF16),
        scratch_shapes=[pltpu.VMEM((VT_ROWS, seq), BF16), pltpu.VMEM((seq, LANES), F32),
                        pltpu.VMEM((2, seq // QC, tk, QC), F32)],
        compiler_params=_params("arbitrary", "arbitrary"),
        name="fox_attention",
    )(proj, proj, proj, fcum)


def _diff_attention(proj, lam_row, g_row, batch, seq, n_heads, tk, out_scale):
    row = pl.BlockSpec((1, LANES), lambda b, h: (0, 0))
    return pl.pallas_call(
        functools.partial(_diff_kernel, tk=tk, out_scale=out_scale),
        grid=(batch, n_heads),
        in_specs=[_head_block(seq, 3 * n_heads), _head_block(seq, 4 * n_heads), _head_block(seq, 5 * n_heads),
                  row, row],
        out_specs=_head_block(seq, 0),
        out_shape=jax.ShapeDtypeStruct((batch * seq, n_heads * HEAD_DIM), BF16),
        scratch_shapes=[pltpu.VMEM((VT_ROWS, seq), BF16), pltpu.VMEM((2, seq // QC, tk, QC), F32),
                        pltpu.VMEM((2, seq, HEAD_DIM), F32)],
        compiler_params=_params("arbitrary", "arbitrary"),
        name="diff_attention",
    )(proj, proj, proj, lam_row, g_row)


ROW_GROUPS = 2


def _outproj_kernel(fo_ref, do_ref, wa_ref, wb_ref, x_ref, gate_ref, lg_ref, lb_ref, o_ref, *, alpha):
    rows = o_ref.shape[0] // ROW_GROUPS
    for g in range(ROW_GROUPS):
        sl = slice(g * rows, (g + 1) * rows)
        y = (jnp.dot(fo_ref[sl, :], wa_ref[...], preferred_element_type=F32)
             + jnp.dot(do_ref[sl, :], wb_ref[...], preferred_element_type=F32))
        z = alpha * x_ref[sl, :] + (1.0 + gate_ref[...]) * y
        o_ref[sl, :] = _layernorm(z, lg_ref[...], lb_ref[...])


def _out_proj(fox_o, diff_o, w_o, layer, x2, mod, gate_base, ln_g, ln_b, tm, rows_per_batch, alpha):
    m, d = x2.shape
    w = fox_o.shape[1]
    bpr = rows_per_batch // tm
    row = pl.BlockSpec((1, d), lambda i: (0, 0))
    return pl.pallas_call(
        functools.partial(_outproj_kernel, alpha=alpha),
        grid=(m // tm,),
        in_specs=[
            pl.BlockSpec((tm, w), lambda i: (i, 0)),
            pl.BlockSpec((tm, w), lambda i: (i, 0)),
            pl.BlockSpec((None, w, d), lambda i: (layer, 0, 0)),
            pl.BlockSpec((None, w, d), lambda i: (layer, 1, 0)),
            pl.BlockSpec((tm, d), lambda i: (i, 0)),
            pl.BlockSpec((None, 1, d), lambda i: (gate_base + i // bpr, 0, 0)),
            row, row,
        ],
        out_specs=pl.BlockSpec((tm, d), lambda i: (i, 0)),
        out_shape=jax.ShapeDtypeStruct((m, d), F32),
        compiler_params=_params("arbitrary"),
        name="out_proj_ln",
    )(fox_o, diff_o, w_o, w_o, x2, mod, ln_g, ln_b)


def _mlp_kernel(x_ref, sc_ref, sh_ref, gate_ref, wu_ref, wd_ref, lg_ref, lb_ref, o_ref, h_ref, *, alpha):
    f = pl.program_id(1)
    last = pl.num_programs(1) - 1

    def hidden(h):
        u = jnp.dot(h, wu_ref[...], preferred_element_type=F32)
        a = jnp.square(jnp.maximum(u, 0.0)).astype(BF16)
        return jnp.dot(a, wd_ref[...], preferred_element_type=F32)

    @pl.when(f == 0)
    def _():
        h = (x_ref[...] * (1.0 + sc_ref[...]) + sh_ref[...]).astype(BF16)
        h_ref[...] = h
        o_ref[...] = hidden(h)

    @pl.when((f > 0) & (f < last))
    def _():
        o_ref[...] += hidden(h_ref[...])

    @pl.when(f == last)
    def _():
        rows = o_ref.shape[0] // ROW_GROUPS
        for g in range(ROW_GROUPS):
            sl = slice(g * rows, (g + 1) * rows)
            y = o_ref[sl, :] + hidden(h_ref[sl, :])
            z = alpha * x_ref[sl, :] + (1.0 + gate_ref[...]) * y
            o_ref[sl, :] = _layernorm(z, lg_ref[...], lb_ref[...])


def _mlp(x2, mod, mod_base, batch, w_up, w_down, layer, ln_g, ln_b, tm, tf, rows_per_batch, alpha):
    m, d = x2.shape
    dff = w_up.shape[2]
    assert dff // tf >= 2
    bpr = rows_per_batch // tm

    def mod_spec(part):
        return pl.BlockSpec((None, 1, d), lambda i, f: (mod_base + part * batch + i // bpr, 0, 0))

    row = pl.BlockSpec((1, d), lambda i, f: (0, 0))
    return pl.pallas_call(
        functools.partial(_mlp_kernel, alpha=alpha),
        grid=(m // tm, dff // tf),
        in_specs=[
            pl.BlockSpec((tm, d), lambda i, f: (i, 0)),
            mod_spec(4), mod_spec(3), mod_spec(5),
            pl.BlockSpec((None, d, tf), lambda i, f: (layer, 0, f)),
            pl.BlockSpec((None, tf, d), lambda i, f: (layer, f, 0)),
            row, row,
        ],
        out_specs=pl.BlockSpec((tm, d), lambda i, f: (i, 0)),
        out_shape=jax.ShapeDtypeStruct((m, d), F32),
        scratch_shapes=[pltpu.VMEM((tm, d), BF16)],
        compiler_params=_params("arbitrary", "arbitrary"),
        name="mlp_ln",
    )(x2, mod, mod, mod, w_up, w_down, ln_g, ln_b)


def kernel(x, c, positions, w_ada, b_ada, w_in, b_f, lambda_q1, lambda_k1, lambda_q2, lambda_k2,
           subln_g, w_o, ln1_g, ln1_b, w_up, w_down, ln2_g, ln2_b):
    batch, seq, d = x.shape
    depth = w_ada.shape[0]
    m = batch * seq
    width = d // 2
    n_heads = width // HEAD_DIM
    assert n_heads <= LANES and d % (2 * HEAD_DIM) == 0
    alpha = (2.0 * depth) ** 0.25

    tk_attn = 2 * QC
    assert seq % tk_attn == 0
    tm_proj = min(1024, seq)
    tm_out = min(512, seq)
    tm_mlp = min(512, seq)
    tf_mlp = min(1024, w_up.shape[2])
    tn_ada = min(1024, 6 * d)
    cum_chunk = min(256, seq)

    c_pad = jnp.pad(c, ((0, -batch % SUBLANES), (0, 0)))
    mod = _ada_mod(c_pad, w_ada, b_ada, tn_ada)[:, :batch]
    mod = mod.reshape(depth, batch, 6, d).transpose(0, 2, 1, 3).reshape(depth * 6 * batch, 1, d)

    lam = _lam_values(lambda_q1, lambda_k1, lambda_q2, lambda_k2)
    tables = _rope_tables(positions, tm_proj)

    w_main, w_f = _w_in_prep(w_in, width, n_heads, min(256, d))
    w_o16, w_up16, w_down16 = w_o.astype(BF16), w_up.astype(BF16), w_down.astype(BF16)

    x2 = x.reshape(m, d)
    for l in range(depth):
        base = l * 6 * batch
        b_f_row = jnp.pad(b_f[l], (0, LANES - n_heads))[None, :]

        proj, ff = _in_proj(x2, mod, base, w_main, w_f, l, tables, tm_proj, seq)
        fcum = _fgate(ff, b_f_row, batch, seq, cum_chunk)
        fox_o = _fox_attention(proj, fcum, batch, seq, n_heads, tk_attn)
        diff_o = _diff_attention(proj, lam[l][None, :], subln_g[l][None, :], batch, seq, n_heads, tk_attn,
                                 1.0 - _lambda_init(l))
        x2 = _out_proj(fox_o, diff_o, w_o16, l, x2, mod, base + 2 * batch,
                       ln1_g[l][None, :], ln1_b[l][None, :], tm_out, seq, alpha)
        x2 = _mlp(x2, mod, base, batch, w_up16, w_down16, l,
                  ln2_g[l][None, :], ln2_b[l][None, :], tm_mlp, tf_mlp, seq, alpha)
    return x2.reshape(batch, seq, d)
```

```python
import functools
import math

import jax
import jax.numpy as jnp
from jax import lax
from jax.experimental import pallas as pl
from jax.experimental.pallas import tpu as pltpu

HEAD_DIM = 128
DIFF_QK_DIM = HEAD_DIM // 2
ROT_DIM = DIFF_QK_DIM // 4
ROT_HALF = ROT_DIM // 2
ROPE_THETA = 500000.0
LN_EPS = 1e-5
RMS_EPS = 1e-5
LANES = 128
SUBLANES = 8
BF16_SUBLANES = 16
QC = 256
VT_ROWS = HEAD_DIM + BF16_SUBLANES
LOG2E = math.log2(math.e)
VMEM_LIMIT_BYTES = 56 * 1024 * 1024

F32 = jnp.float32
BF16 = jnp.bfloat16


def _params(*semantics):
    return pltpu.CompilerParams(dimension_semantics=semantics, vmem_limit_bytes=VMEM_LIMIT_BYTES)


def _lambda_init(layer_idx):
    return 0.8 - 0.6 * math.exp(-0.3 * layer_idx)


def _layernorm(z, g, b):
    mu = jnp.mean(z, axis=-1, keepdims=True)
    zc = z - mu
    var = jnp.mean(zc * zc, axis=-1, keepdims=True)
    return zc * lax.rsqrt(var + LN_EPS) * g + b


def _ada_kernel(c_ref, w_ref, b_ref, o_ref):
    c = c_ref[...]
    c_act = (c * jax.nn.sigmoid(c)).astype(BF16)
    o_ref[...] = jnp.dot(c_act, w_ref[...].astype(BF16), preferred_element_type=F32) + b_ref[...]


def _ada_mod(c_pad, w_ada, b_ada, tn):
    depth, d, n = w_ada.shape
    rows = c_pad.shape[0]
    return pl.pallas_call(
        _ada_kernel,
        grid=(depth, n // tn),
        in_specs=[
            pl.BlockSpec((rows, d), lambda l, j: (0, 0)),
            pl.BlockSpec((None, d, tn), lambda l, j: (l, 0, j)),
            pl.BlockSpec((None, 1, tn), lambda l, j: (l, 0, j)),
        ],
        out_specs=pl.BlockSpec((None, rows, tn), lambda l, j: (l, 0, j)),
        out_shape=jax.ShapeDtypeStruct((depth, rows, n), F32),
        compiler_params=_params("arbitrary", "arbitrary"),
        name="ada_mod",
    )(c_pad, w_ada, b_ada.reshape(depth, 1, n))


def _lam_kernel(q1_ref, k1_ref, q2_ref, k2_ref, init_ref, o_ref):
    s1 = jnp.sum(q1_ref[...] * k1_ref[...], axis=-1, keepdims=True)
    s2 = jnp.sum(q2_ref[...] * k2_ref[...], axis=-1, keepdims=True)
    lam = jnp.exp(s1) - jnp.exp(s2) + init_ref[...]
    o_ref[...] = jnp.broadcast_to(lam, o_ref.shape)


def _lam_values(lq1, lk1, lq2, lk2):
    depth = lq1.shape[0]
    init = jnp.asarray([[_lambda_init(l)] for l in range(depth)], F32)
    return pl.pallas_call(
        _lam_kernel,
        out_shape=jax.ShapeDtypeStruct((depth, LANES), F32),
        name="lam_values",
    )(lq1, lk1, lq2, lk2, init)


def _rope_table_kernel(pos_ref, invf_ref, c_ref, s_ref):
    ang = pos_ref[...].astype(F32) * invf_ref[...]
    lane = lax.broadcasted_iota(jnp.int32, ang.shape, 1)
    sin = jnp.sin(ang)
    c_ref[...] = jnp.cos(ang)
    s_ref[...] = jnp.where(lane < LANES // 2, -sin, sin)


def _rope_tables(positions, tm):
    m = positions.size
    inv_freq = ROPE_THETA ** (-jnp.arange(0, ROT_DIM, 2, dtype=F32) / ROT_DIM)
    lane = jnp.arange(LANES) % (LANES // 2)
    invf_row = jnp.where(lane < ROT_DIM, inv_freq[lane % ROT_HALF], 0.0).astype(F32)[None, :]
    spec = pl.BlockSpec((tm, LANES), lambda i: (i, 0))
    shape = jax.ShapeDtypeStruct((m, LANES), F32)
    return pl.pallas_call(
        _rope_table_kernel,
        grid=(m // tm,),
        in_specs=[pl.BlockSpec((tm, 1), lambda i: (i, 0)), pl.BlockSpec((1, LANES), lambda i: (0, 0))],
        out_specs=[spec, spec],
        out_shape=[shape, shape],
        compiler_params=_params("arbitrary"),
        name="rope_tables",
    )(positions.reshape(m, 1), invf_row)


def _diff_head_feature(lane):
    swap = DIFF_QK_DIM - ROT_HALF
    return jnp.where((lane >= ROT_HALF) & (lane < ROT_DIM), lane + swap,
                     jnp.where((lane >= DIFF_QK_DIM) & (lane < DIFF_QK_DIM + ROT_HALF), lane - swap, lane))


def _w_in_prep_kernel(wt_ref, main_ref, f_ref, *, width, n_gate):
    gate0 = 3 * width
    diff0 = gate0 + n_gate
    heads = width // HEAD_DIM

    def piece(col0):
        return wt_ref[col0:col0 + HEAD_DIM, :].T.astype(BF16)

    for hd in range(3 * heads):
        main_ref[:, hd * HEAD_DIM:(hd + 1) * HEAD_DIM] = piece(hd * HEAD_DIM)
    lane = lax.broadcasted_iota(jnp.int32, f_ref.shape, 1)
    f_ref[...] = jnp.where(lane < n_gate, piece(gate0), jnp.zeros(f_ref.shape, BF16))
    src = lax.broadcasted_iota(jnp.int32, (HEAD_DIM, HEAD_DIM), 0)
    dst = lax.broadcasted_iota(jnp.int32, (HEAD_DIM, HEAD_DIM), 1)
    reorder = jnp.where(src == _diff_head_feature(dst), 1.0, 0.0).astype(BF16)
    for hd in range(3 * heads):
        p = piece(diff0 + hd * HEAD_DIM)
        if hd < 2 * heads:
            p = jnp.dot(p, reorder, preferred_element_type=F32).astype(BF16)
        main_ref[:, gate0 + hd * HEAD_DIM:gate0 + (hd + 1) * HEAD_DIM] = p


def _w_in_prep(w_in, width, n_gate, rows):
    depth, d, cols = w_in.shape
    return pl.pallas_call(
        functools.partial(_w_in_prep_kernel, width=width, n_gate=n_gate),
        grid=(depth, d // rows),
        in_specs=[pl.BlockSpec((None, cols, rows), lambda l, i: (l, 0, i))],
        out_specs=[pl.BlockSpec((None, rows, 6 * width), lambda l, i: (l, i, 0)),
                   pl.BlockSpec((None, rows, LANES), lambda l, i: (l, i, 0))],
        out_shape=[jax.ShapeDtypeStruct((depth, d, 6 * width), BF16), jax.ShapeDtypeStruct((depth, d, LANES), BF16)],
        compiler_params=_params("arbitrary", "arbitrary"),
        name="w_in_prep",
    )(jnp.swapaxes(w_in, 1, 2))


def _first_map_features(idx):
    return (idx < ROT_HALF) | ((idx >= ROT_DIM) & (idx < DIFF_QK_DIM + ROT_HALF))


def _inproj_kernel(x_ref, sc_ref, sh_ref, w_ref, wf_ref, c_ref, s_ref, proj_ref, ff_ref, h_ref,
                   *, fox_scale, diff_scale):
    j = pl.program_id(1)

    def project(h, c_eff, s_eff):
        acc = jnp.dot(h, w_ref[...], preferred_element_type=F32)
        for hd in range(acc.shape[1] // LANES):
            xs = acc[:, hd * LANES:(hd + 1) * LANES]
            partner = pltpu.roll(xs, LANES // 2, 1)
            proj_ref[:, hd * LANES:(hd + 1) * LANES] = (xs * c_eff + partner * s_eff).astype(proj_ref.dtype)

    @pl.when(j == 0)
    def _():
        h = (x_ref[...] * (1.0 + sc_ref[...]) + sh_ref[...]).astype(BF16)
        h_ref[...] = h
        ff_ref[...] = jnp.dot(h, wf_ref[...], preferred_element_type=F32)
        project(h, jnp.full(c_ref.shape, fox_scale, F32), jnp.zeros(s_ref.shape, F32))

    @pl.when(j > 0)
    def _():
        rotary = (j == 3) | (j == 4)
        scale = jnp.where(j == 3, diff_scale, 1.0).astype(F32)
        project(h_ref[...], jnp.where(rotary, c_ref[...], 1.0) * scale, jnp.where(rotary, s_ref[...], 0.0) * scale)


def _in_proj(x2, mod, mod_base, w_main, w_f, layer, tables, tm, rows_per_batch):
    m, d = x2.shape
    w = w_main.shape[2] // 6
    bpr = rows_per_batch // tm
    batch = m // rows_per_batch

    def mod_spec(part):
        return pl.BlockSpec((None, 1, d), lambda i, j: (mod_base + part * batch + i // bpr, 0, 0))

    tab_spec = pl.BlockSpec((tm, LANES), lambda i, j: (i, 0))
    kernel = functools.partial(_inproj_kernel, fox_scale=HEAD_DIM ** -0.5 * LOG2E,
                               diff_scale=DIFF_QK_DIM ** -0.5 * LOG2E)
    return pl.pallas_call(
        kernel,
        grid=(m // tm, 6),
        in_specs=[
            pl.BlockSpec((tm, d), lambda i, j: (i, 0)),
            mod_spec(1), mod_spec(0),
            pl.BlockSpec((None, d, w), lambda i, j: (layer, 0, j)),
            pl.BlockSpec((None, d, LANES), lambda i, j: (layer, 0, 0)),
            tab_spec, tab_spec,
        ],
        out_specs=[
            pl.BlockSpec((tm, w), lambda i, j: (i, j)),
            pl.BlockSpec((tm, LANES), lambda i, j: (i, 0)),
        ],
        out_shape=[jax.ShapeDtypeStruct((m, 6 * w), BF16), jax.ShapeDtypeStruct((m, LANES), F32)],
        scratch_shapes=[pltpu.VMEM((tm, d), BF16)],
        compiler_params=_params("arbitrary", "arbitrary"),
        name="in_proj",
    )(x2, mod, mod, w_main, w_f, *tables)


def _fgate_kernel(ff_ref, bf_ref, o_ref, *, chunk):
    s = ff_ref.shape[0]
    row = lax.broadcasted_iota(jnp.int32, (chunk, chunk), 0)
    col = lax.broadcasted_iota(jnp.int32, (chunk, chunk), 1)
    tri = (col <= row).astype(BF16)
    carry = jnp.zeros((1, LANES), F32)
    for ci in range(s // chunk):
        x = ff_ref[ci * chunk:(ci + 1) * chunk, :] + bf_ref[...]
        lf = jnp.minimum(x, 0.0) - jnp.log1p(jnp.exp(-jnp.abs(x)))
        hi = lf.astype(BF16)
        r1 = lf - hi.astype(F32)
        mid = r1.astype(BF16)
        lo = (r1 - mid.astype(F32)).astype(BF16)
        cs = (jnp.dot(tri, hi, preferred_element_type=F32)
              + jnp.dot(tri, mid, preferred_element_type=F32)
              + jnp.dot(tri, lo, preferred_element_type=F32)) + carry
        carry = cs[chunk - 1:chunk, :]
        o_ref[ci * chunk:(ci + 1) * chunk, :] = -cs


def _fgate(ff, b_f_row, batch, seq, chunk):
    return pl.pallas_call(
        functools.partial(_fgate_kernel, chunk=chunk),
        grid=(batch,),
        in_specs=[pl.BlockSpec((seq, LANES), lambda b: (b, 0)), pl.BlockSpec((1, LANES), lambda b: (0, 0))],
        out_specs=pl.BlockSpec((seq, LANES), lambda b: (b, 0)),
        out_shape=jax.ShapeDtypeStruct((batch * seq, LANES), F32),
        compiler_params=_params("arbitrary"),
        name="fgate",
    )(ff, b_f_row)


def _fill_vt(v_ref, vt_ref, chunk):
    seq = v_ref.shape[0]
    for c in range(seq // chunk):
        blk = v_ref[c * chunk:(c + 1) * chunk, :].astype(F32)
        vt_ref[0:HEAD_DIM, c * chunk:(c + 1) * chunk] = blk.T.astype(BF16)
    row = lax.broadcasted_iota(jnp.int32, (VT_ROWS - HEAD_DIM, seq), 0)
    vt_ref[HEAD_DIM:VT_ROWS, :] = jnp.where(row == 0, 1.0, 0.0).astype(BF16)


def _chain_step(s, m, acc, vt_blk):
    m_new = jnp.maximum(m, jnp.max(s, axis=0, keepdims=True))
    m_safe = jnp.where(m_new == -jnp.inf, 0.0, m_new)
    alpha = jnp.exp2(m - m_safe)
    p = jnp.exp2(s - m_safe).astype(BF16)
    acc_new = alpha * acc + jnp.dot(vt_blk, p, preferred_element_type=F32)
    return m_new, acc_new


def _key_le_query(keys, queries, key0, query0):
    kk = lax.broadcasted_iota(jnp.int32, (keys, queries), 0) + key0
    qq = lax.broadcasted_iota(jnp.int32, (keys, queries), 1) + query0
    return kk <= qq


def _finish(acc):
    o_t = acc[0:HEAD_DIM, :] / acc[HEAD_DIM:HEAD_DIM + 1, :]
    return o_t.T


def _keys_seen(kb, c, tk):
    return min(tk, (c + 1) * QC - kb * tk)


def _causal_pipeline(n_kb, n_chains, per_kb, issue_scores, consume, init):
    state = init
    issue_scores(0, 0, tuple(range(n_chains)))
    for kb in range(n_kb):
        if kb + 1 < n_kb:
            issue_scores((kb + 1) % 2, kb + 1, tuple(range(per_kb * (kb + 1), n_chains)))
        state = consume(kb % 2, kb, tuple(range(per_kb * kb, n_chains)), state)
    return state


def _make_consume(s_ref, vt_ref, tk):
    per_kb = tk // QC

    def consume(slot, kb, chains, state):
        state = list(state)
        for c in chains:
            keys = _keys_seen(kb, c, tk)
            s = s_ref[slot, c, 0:keys, :]
            if c < per_kb * (kb + 1):
                s = jnp.where(_key_le_query(keys, QC, 0, keys - QC), s, -jnp.inf)
            state[c] = _chain_step(s, state[c][0], state[c][1], vt_ref[:, kb * tk:kb * tk + keys])
        return tuple(state)
    return consume


def _init_state(n_chains):
    return tuple((jnp.full((1, QC), -jnp.inf, F32), jnp.zeros((VT_ROWS, QC), F32)) for _ in range(n_chains))


def _fox_kernel(q_ref, k_ref, v_ref, f_ref, w32_ref, o_ref, w16_ref, vt_ref, bcol_ref, s_ref, *, tk):
    h = pl.program_id(1)
    w16_ref[...] = w32_ref[...].astype(BF16)
    seq = k_ref.shape[0]
    nq = seq // QC
    _fill_vt(v_ref, vt_ref, tk)
    lane = lax.broadcasted_iota(jnp.int32, (seq, LANES), 1)
    col = jnp.sum(jnp.where(lane == h, f_ref[...], 0.0), axis=1, keepdims=True)
    bcol_ref[...] = jnp.broadcast_to(col * LOG2E, (seq, LANES))
    q_cs = [q_ref[c * QC:(c + 1) * QC, :].astype(F32).T.astype(BF16) for c in range(nq)]

    def issue_scores(slot, kb, chains):
        for c in chains:
            keys = _keys_seen(kb, c, tk)
            b = bcol_ref[kb * tk:kb * tk + keys, :]
            s = jnp.dot(k_ref[kb * tk:kb * tk + keys, :], q_cs[c], preferred_element_type=F32)
            for u in range(QC // LANES):
                s_ref[slot, c, 0:keys, u * LANES:(u + 1) * LANES] = s[:, u * LANES:(u + 1) * LANES] + b

    state = _causal_pipeline(seq // tk, nq, tk // QC, issue_scores, _make_consume(s_ref, vt_ref, tk),
                             _init_state(nq))
    for c in range(nq):
        o_ref[c * QC:(c + 1) * QC, :] = _finish(state[c][1]).astype(o_ref.dtype)


def _diff_kernel(q_ref, k_ref, v_ref, lam_ref, g_ref, w32_ref, o_ref, w16_ref, vt_ref, s_ref, om_ref,
                 *, tk, out_scale):
    w16_ref[...] = w32_ref[...].astype(BF16)
    seq = k_ref.shape[0]
    nq = seq // QC
    _fill_vt(v_ref, vt_ref, tk)
    q_ts = [q_ref[c * QC:(c + 1) * QC, :].astype(F32).T for c in range(nq)]
    first = _first_map_features(lax.broadcasted_iota(jnp.int32, (HEAD_DIM, QC), 0))
    consume = _make_consume(s_ref, vt_ref, tk)

    def one_map(r, carry):
        keep = first == (r == 0)
        q_cs = [jnp.where(keep, q, 0.0).astype(BF16) for q in q_ts]

        def issue_scores(slot, kb, chains):
            for c in chains:
                keys = _keys_seen(kb, c, tk)
                s_ref[slot, c, 0:keys, :] = jnp.dot(k_ref[kb * tk:kb * tk + keys, :], q_cs[c],
                                                    preferred_element_type=F32)

        state = _causal_pipeline(seq // tk, nq, tk // QC, issue_scores, consume, _init_state(nq))
        for c in range(nq):
            om_ref[r, c * QC:(c + 1) * QC, :] = _finish(state[c][1])
        return carry

    lax.fori_loop(0, 2, one_map, 0)
    o = om_ref[0] - lam_ref[...] * om_ref[1]
    o = o * lax.rsqrt(jnp.mean(o * o, axis=-1, keepdims=True) + RMS_EPS)
    o_ref[...] = (o * g_ref[...] * out_scale).astype(o_ref.dtype)


def _head_block(seq, col0):
    return pl.BlockSpec((seq, HEAD_DIM), lambda b, h: (b, col0 + h))


def _ride_along(w, layer, batch, n_heads):
    rows = w.shape[1] // (batch * n_heads)
    assert rows * batch * n_heads == w.shape[1] and rows % BF16_SUBLANES == 0
    spec_in = pl.BlockSpec((None, rows, w.shape[2]), lambda b, h: (layer, b * n_heads + h, 0))
    spec_out = pl.BlockSpec((rows, w.shape[2]), lambda b, h: (b * n_heads + h, 0))
    return spec_in, spec_out, jax.ShapeDtypeStruct(w.shape[1:], BF16)


def _fox_attention(proj, fcum, w32, layer, batch, seq, n_heads, tk):
    w_in_spec, w_out_spec, w_shape = _ride_along(w32, layer, batch, n_heads)
    return pl.pallas_call(
        functools.partial(_fox_kernel, tk=tk),
        grid=(batch, n_heads),
        in_specs=[_head_block(seq, 0), _head_block(seq, n_heads), _head_block(seq, 2 * n_heads),
                  pl.BlockSpec((seq, LANES), lambda b, h: (b, 0)), w_in_spec],
        out_specs=[_head_block(seq, 0), w_out_spec],
        out_shape=[jax.ShapeDtypeStruct((batch * seq, n_heads * HEAD_DIM), BF16), w_shape],
        scratch_shapes=[pltpu.VMEM((VT_ROWS, seq), BF16), pltpu.VMEM((seq, LANES), F32),
                        pltpu.VMEM((2, seq // QC, tk, QC), F32)],
        compiler_params=_params("arbitrary", "arbitrary"),
        name="fox_attention",
    )(proj, proj, proj, fcum, w32)


def _diff_attention(proj, lam_row, g_row, w32, layer, batch, seq, n_heads, tk, out_scale):
    row = pl.BlockSpec((1, LANES), lambda b, h: (0, 0))
    w_in_spec, w_out_spec, w_shape = _ride_along(w32, layer, batch, n_heads)
    return pl.pallas_call(
        functools.partial(_diff_kernel, tk=tk, out_scale=out_scale),
        grid=(batch, n_heads),
        in_specs=[_head_block(seq, 3 * n_heads), _head_block(seq, 4 * n_heads), _head_block(seq, 5 * n_heads),
                  row, row, w_in_spec],
        out_specs=[_head_block(seq, 0), w_out_spec],
        out_shape=[jax.ShapeDtypeStruct((batch * seq, n_heads * HEAD_DIM), BF16), w_shape],
        scratch_shapes=[pltpu.VMEM((VT_ROWS, seq), BF16), pltpu.VMEM((2, seq // QC, tk, QC), F32),
                        pltpu.VMEM((2, seq, HEAD_DIM), F32)],
        compiler_params=_params("arbitrary", "arbitrary"),
        name="diff_attention",
    )(proj, proj, proj, lam_row, g_row, w32)


ROW_GROUPS = 2


def _outproj_kernel(fo_ref, do_ref, wa_ref, wb_ref, x_ref, gate_ref, lg_ref, lb_ref, o_ref, *, alpha):
    rows = o_ref.shape[0] // ROW_GROUPS
    for g in range(ROW_GROUPS):
        sl = slice(g * rows, (g + 1) * rows)
        y = (jnp.dot(fo_ref[sl, :], wa_ref[...], preferred_element_type=F32)
             + jnp.dot(do_ref[sl, :], wb_ref[...], preferred_element_type=F32))
        z = alpha * x_ref[sl, :] + (1.0 + gate_ref[...]) * y
        o_ref[sl, :] = _layernorm(z, lg_ref[...], lb_ref[...])


def _out_proj(fox_o, diff_o, w_o, layer, x2, mod, gate_base, ln_g, ln_b, tm, rows_per_batch, alpha):
    m, d = x2.shape
    w = fox_o.shape[1]
    bpr = rows_per_batch // tm
    row = pl.BlockSpec((1, d), lambda i: (0, 0))
    return pl.pallas_call(
        functools.partial(_outproj_kernel, alpha=alpha),
        grid=(m // tm,),
        in_specs=[
            pl.BlockSpec((tm, w), lambda i: (i, 0)),
            pl.BlockSpec((tm, w), lambda i: (i, 0)),
            pl.BlockSpec((None, w, d), lambda i: (layer, 0, 0)),
            pl.BlockSpec((None, w, d), lambda i: (layer, 1, 0)),
            pl.BlockSpec((tm, d), lambda i: (i, 0)),
            pl.BlockSpec((None, 1, d), lambda i: (gate_base + i // bpr, 0, 0)),
            row, row,
        ],
        out_specs=pl.BlockSpec((tm, d), lambda i: (i, 0)),
        out_shape=jax.ShapeDtypeStruct((m, d), F32),
        compiler_params=_params("arbitrary"),
        name="out_proj_ln",
    )(fox_o, diff_o, w_o, w_o, x2, mod, ln_g, ln_b)


def _mlp_kernel(x_ref, sc_ref, sh_ref, gate_ref, wu_ref, wd_ref, lg_ref, lb_ref, o_ref, h_ref, *, alpha):
    f = pl.program_id(1)
    last = pl.num_programs(1) - 1

    def hidden(h):
        u = jnp.dot(h, wu_ref[...], preferred_element_type=F32)
        a = jnp.square(jnp.maximum(u, 0.0)).astype(BF16)
        return jnp.dot(a, wd_ref[...], preferred_element_type=F32)

    @pl.when(f == 0)
    def _():
        h = (x_ref[...] * (1.0 + sc_ref[...]) + sh_ref[...]).astype(BF16)
        h_ref[...] = h
        o_ref[...] = hidden(h)

    @pl.when((f > 0) & (f < last))
    def _():
        o_ref[...] += hidden(h_ref[...])

    @pl.when(f == last)
    def _():
        rows = o_ref.shape[0] // ROW_GROUPS
        for g in range(ROW_GROUPS):
            sl = slice(g * rows, (g + 1) * rows)
            y = o_ref[sl, :] + hidden(h_ref[sl, :])
            z = alpha * x_ref[sl, :] + (1.0 + gate_ref[...]) * y
            o_ref[sl, :] = _layernorm(z, lg_ref[...], lb_ref[...])


def _mlp(x2, mod, mod_base, batch, w_up, w_down, ln_g, ln_b, tm, tf, rows_per_batch, alpha):
    m, d = x2.shape
    dff = w_up.shape[1]
    assert dff // tf >= 2
    bpr = rows_per_batch // tm

    def mod_spec(part):
        return pl.BlockSpec((None, 1, d), lambda i, f: (mod_base + part * batch + i // bpr, 0, 0))

    row = pl.BlockSpec((1, d), lambda i, f: (0, 0))
    return pl.pallas_call(
        functools.partial(_mlp_kernel, alpha=alpha),
        grid=(m // tm, dff // tf),
        in_specs=[
            pl.BlockSpec((tm, d), lambda i, f: (i, 0)),
            mod_spec(4), mod_spec(3), mod_spec(5),
            pl.BlockSpec((d, tf), lambda i, f: (0, f)),
            pl.BlockSpec((tf, d), lambda i, f: (f, 0)),
            row, row,
        ],
        out_specs=pl.BlockSpec((tm, d), lambda i, f: (i, 0)),
        out_shape=jax.ShapeDtypeStruct((m, d), F32),
        scratch_shapes=[pltpu.VMEM((tm, d), BF16)],
        compiler_params=_params("arbitrary", "arbitrary"),
        name="mlp_ln",
    )(x2, mod, mod, mod, w_up, w_down, ln_g, ln_b)


def kernel(x, c, positions, w_ada, b_ada, w_in, b_f, lambda_q1, lambda_k1, lambda_q2, lambda_k2,
           subln_g, w_o, ln1_g, ln1_b, w_up, w_down, ln2_g, ln2_b):
    batch, seq, d = x.shape
    depth = w_ada.shape[0]
    m = batch * seq
    width = d // 2
    n_heads = width // HEAD_DIM
    assert n_heads <= LANES and d % (2 * HEAD_DIM) == 0
    alpha = (2.0 * depth) ** 0.25

    tk_attn = 2 * QC
    assert seq % tk_attn == 0
    tm_proj = min(1024, seq)
    tm_out = min(512, seq)
    tm_mlp = min(512, seq)
    tf_mlp = min(1024, w_up.shape[2])
    tn_ada = min(1024, 6 * d)
    cum_chunk = min(256, seq)

    c_pad = jnp.pad(c, ((0, -batch % SUBLANES), (0, 0)))
    mod = _ada_mod(c_pad, w_ada, b_ada, tn_ada)[:, :batch]
    mod = mod.reshape(depth, batch, 6, d).transpose(0, 2, 1, 3).reshape(depth * 6 * batch, 1, d)

    lam = _lam_values(lambda_q1, lambda_k1, lambda_q2, lambda_k2)
    tables = _rope_tables(positions, tm_proj)

    w_main, w_f = _w_in_prep(w_in, width, n_heads, min(256, d))
    w_o16 = w_o.astype(BF16)

    x2 = x.reshape(m, d)
    for l in range(depth):
        base = l * 6 * batch
        b_f_row = jnp.pad(b_f[l], (0, LANES - n_heads))[None, :]

        proj, ff = _in_proj(x2, mod, base, w_main, w_f, l, tables, tm_proj, seq)
        fcum = _fgate(ff, b_f_row, batch, seq, cum_chunk)
        fox_o, w_up16 = _fox_attention(proj, fcum, w_up, l, batch, seq, n_heads, tk_attn)
        diff_o, w_down16 = _diff_attention(proj, lam[l][None, :], subln_g[l][None, :], w_down, l, batch, seq,
                                           n_heads, tk_attn, 1.0 - _lambda_init(l))
        x2 = _out_proj(fox_o, diff_o, w_o16, l, x2, mod, base + 2 * batch,
                       ln1_g[l][None, :], ln1_b[l][None, :], tm_out, seq, alpha)
        x2 = _mlp(x2, mod, base, batch, w_up16, w_down16,
                  ln2_g[l][None, :], ln2_b[l][None, :], tm_mlp, tf_mlp, seq, alpha)
    return x2.reshape(batch, seq, d)
```

```python
import functools
import math

import jax
import jax.numpy as jnp
from jax import lax
from jax.experimental import pallas as pl
from jax.experimental.pallas import tpu as pltpu

HEAD_DIM = 128
DIFF_QK_DIM = HEAD_DIM // 2
ROT_DIM = DIFF_QK_DIM // 4
ROT_HALF = ROT_DIM // 2
ROPE_THETA = 500000.0
LN_EPS = 1e-5
RMS_EPS = 1e-5
LANES = 128
SUBLANES = 8
BF16_SUBLANES = 16
QC = 256
VT_ROWS = HEAD_DIM + BF16_SUBLANES
LOG2E = math.log2(math.e)
VMEM_LIMIT_BYTES = 56 * 1024 * 1024

F32 = jnp.float32
BF16 = jnp.bfloat16


def _params(*semantics):
    return pltpu.CompilerParams(dimension_semantics=semantics, vmem_limit_bytes=VMEM_LIMIT_BYTES)


def _lambda_init(layer_idx):
    return 0.8 - 0.6 * math.exp(-0.3 * layer_idx)


def _layernorm(z, g, b):
    mu = jnp.mean(z, axis=-1, keepdims=True)
    zc = z - mu
    var = jnp.mean(zc * zc, axis=-1, keepdims=True)
    return zc * lax.rsqrt(var + LN_EPS) * g + b


def _ada_kernel(c_ref, w_ref, b_ref, o_ref):
    c = c_ref[...]
    c_act = (c * jax.nn.sigmoid(c)).astype(BF16)
    o_ref[...] = jnp.dot(c_act, w_ref[...].astype(BF16), preferred_element_type=F32) + b_ref[...]


def _ada_mod(c_pad, w_ada, b_ada, tn):
    depth, d, n = w_ada.shape
    rows = c_pad.shape[0]
    return pl.pallas_call(
        _ada_kernel,
        grid=(depth, n // tn),
        in_specs=[
            pl.BlockSpec((rows, d), lambda l, j: (0, 0)),
            pl.BlockSpec((None, d, tn), lambda l, j: (l, 0, j)),
            pl.BlockSpec((None, 1, tn), lambda l, j: (l, 0, j)),
        ],
        out_specs=pl.BlockSpec((None, rows, tn), lambda l, j: (l, 0, j)),
        out_shape=jax.ShapeDtypeStruct((depth, rows, n), F32),
        compiler_params=_params("arbitrary", "arbitrary"),
        name="ada_mod",
    )(c_pad, w_ada, b_ada.reshape(depth, 1, n))


def _lam_kernel(q1_ref, k1_ref, q2_ref, k2_ref, init_ref, o_ref):
    s1 = jnp.sum(q1_ref[...] * k1_ref[...], axis=-1, keepdims=True)
    s2 = jnp.sum(q2_ref[...] * k2_ref[...], axis=-1, keepdims=True)
    lam = jnp.exp(s1) - jnp.exp(s2) + init_ref[...]
    o_ref[...] = jnp.broadcast_to(lam, o_ref.shape)


def _lam_values(lq1, lk1, lq2, lk2):
    depth = lq1.shape[0]
    init = jnp.asarray([[_lambda_init(l)] for l in range(depth)], F32)
    return pl.pallas_call(
        _lam_kernel,
        out_shape=jax.ShapeDtypeStruct((depth, LANES), F32),
        name="lam_values",
    )(lq1, lk1, lq2, lk2, init)


def _rope_table_kernel(pos_ref, invf_ref, c_ref, s_ref):
    ang = pos_ref[...].astype(F32) * invf_ref[...]
    lane = lax.broadcasted_iota(jnp.int32, ang.shape, 1)
    sin = jnp.sin(ang)
    c_ref[...] = jnp.cos(ang)
    s_ref[...] = jnp.where(lane < LANES // 2, -sin, sin)


def _rope_tables(positions, tm):
    m = positions.size
    inv_freq = ROPE_THETA ** (-jnp.arange(0, ROT_DIM, 2, dtype=F32) / ROT_DIM)
    lane = jnp.arange(LANES) % (LANES // 2)
    invf_row = jnp.where(lane < ROT_DIM, inv_freq[lane % ROT_HALF], 0.0).astype(F32)[None, :]
    spec = pl.BlockSpec((tm, LANES), lambda i: (i, 0))
    shape = jax.ShapeDtypeStruct((m, LANES), F32)
    return pl.pallas_call(
        _rope_table_kernel,
        grid=(m // tm,),
        in_specs=[pl.BlockSpec((tm, 1), lambda i: (i, 0)), pl.BlockSpec((1, LANES), lambda i: (0, 0))],
        out_specs=[spec, spec],
        out_shape=[shape, shape],
        compiler_params=_params("arbitrary"),
        name="rope_tables",
    )(positions.reshape(m, 1), invf_row)


def _diff_head_feature(lane):
    swap = DIFF_QK_DIM - ROT_HALF
    return jnp.where((lane >= ROT_HALF) & (lane < ROT_DIM), lane + swap,
                     jnp.where((lane >= DIFF_QK_DIM) & (lane < DIFF_QK_DIM + ROT_HALF), lane - swap, lane))


def _w_in_prep_kernel(wt_ref, main_ref, f_ref, *, width, n_gate):
    gate0 = 3 * width
    diff0 = gate0 + n_gate
    heads = width // HEAD_DIM

    def piece(col0):
        return wt_ref[col0:col0 + HEAD_DIM, :].T.astype(BF16)

    for hd in range(3 * heads):
        main_ref[:, hd * HEAD_DIM:(hd + 1) * HEAD_DIM] = piece(hd * HEAD_DIM)
    lane = lax.broadcasted_iota(jnp.int32, f_ref.shape, 1)
    f_ref[...] = jnp.where(lane < n_gate, piece(gate0), jnp.zeros(f_ref.shape, BF16))
    src = lax.broadcasted_iota(jnp.int32, (HEAD_DIM, HEAD_DIM), 0)
    dst = lax.broadcasted_iota(jnp.int32, (HEAD_DIM, HEAD_DIM), 1)
    reorder = jnp.where(src == _diff_head_feature(dst), 1.0, 0.0).astype(BF16)
    for hd in range(3 * heads):
        p = piece(diff0 + hd * HEAD_DIM)
        if hd < 2 * heads:
            p = jnp.dot(p, reorder, preferred_element_type=F32).astype(BF16)
        main_ref[:, gate0 + hd * HEAD_DIM:gate0 + (hd + 1) * HEAD_DIM] = p


def _w_in_prep(w_in, width, n_gate, rows):
    depth, d, cols = w_in.shape
    return pl.pallas_call(
        functools.partial(_w_in_prep_kernel, width=width, n_gate=n_gate),
        grid=(depth, d // rows),
        in_specs=[pl.BlockSpec((None, cols, rows), lambda l, i: (l, 0, i))],
        out_specs=[pl.BlockSpec((None, rows, 6 * width), lambda l, i: (l, i, 0)),
                   pl.BlockSpec((None, rows, LANES), lambda l, i: (l, i, 0))],
        out_shape=[jax.ShapeDtypeStruct((depth, d, 6 * width), BF16), jax.ShapeDtypeStruct((depth, d, LANES), BF16)],
        compiler_params=_params("arbitrary", "arbitrary"),
        name="w_in_prep",
    )(jnp.swapaxes(w_in, 1, 2))


def _first_map_features(idx):
    return (idx < ROT_HALF) | ((idx >= ROT_DIM) & (idx < DIFF_QK_DIM + ROT_HALF))


def _inproj_kernel(x_ref, sc_ref, sh_ref, w_ref, wf_ref, c_ref, s_ref, proj_ref, ff_ref, h_ref,
                   *, fox_scale, diff_scale):
    j = pl.program_id(1)

    def project(h, c_eff, s_eff):
        acc = jnp.dot(h, w_ref[...], preferred_element_type=F32)
        for hd in range(acc.shape[1] // LANES):
            xs = acc[:, hd * LANES:(hd + 1) * LANES]
            partner = pltpu.roll(xs, LANES // 2, 1)
            proj_ref[:, hd * LANES:(hd + 1) * LANES] = (xs * c_eff + partner * s_eff).astype(proj_ref.dtype)

    @pl.when(j == 0)
    def _():
        h = (x_ref[...] * (1.0 + sc_ref[...]) + sh_ref[...]).astype(BF16)
        h_ref[...] = h
        ff_ref[...] = jnp.dot(h, wf_ref[...], preferred_element_type=F32)
        project(h, jnp.full(c_ref.shape, fox_scale, F32), jnp.zeros(s_ref.shape, F32))

    @pl.when(j > 0)
    def _():
        rotary = (j == 3) | (j == 4)
        scale = jnp.where(j == 3, diff_scale, 1.0).astype(F32)
        project(h_ref[...], jnp.where(rotary, c_ref[...], 1.0) * scale, jnp.where(rotary, s_ref[...], 0.0) * scale)


def _in_proj(x2, mod, mod_base, w_main, w_f, layer, tables, tm, rows_per_batch):
    m, d = x2.shape
    w = w_main.shape[2] // 6
    bpr = rows_per_batch // tm
    batch = m // rows_per_batch

    def mod_spec(part):
        return pl.BlockSpec((None, 1, d), lambda i, j: (mod_base + part * batch + i // bpr, 0, 0))

    tab_spec = pl.BlockSpec((tm, LANES), lambda i, j: (i, 0))
    kernel = functools.partial(_inproj_kernel, fox_scale=HEAD_DIM ** -0.5 * LOG2E,
                               diff_scale=DIFF_QK_DIM ** -0.5 * LOG2E)
    return pl.pallas_call(
        kernel,
        grid=(m // tm, 6),
        in_specs=[
            pl.BlockSpec((tm, d), lambda i, j: (i, 0)),
            mod_spec(1), mod_spec(0),
            pl.BlockSpec((None, d, w), lambda i, j: (layer, 0, j)),
            pl.BlockSpec((None, d, LANES), lambda i, j: (layer, 0, 0)),
            tab_spec, tab_spec,
        ],
        out_specs=[
            pl.BlockSpec((tm, w), lambda i, j: (i, j)),
            pl.BlockSpec((tm, LANES), lambda i, j: (i, 0)),
        ],
        out_shape=[jax.ShapeDtypeStruct((m, 6 * w), BF16), jax.ShapeDtypeStruct((m, LANES), F32)],
        scratch_shapes=[pltpu.VMEM((tm, d), BF16)],
        compiler_params=_params("arbitrary", "arbitrary"),
        name="in_proj",
    )(x2, mod, mod, w_main, w_f, *tables)


def _fgate_kernel(ff_ref, bf_ref, o_ref, *, chunk):
    s = ff_ref.shape[0]
    row = lax.broadcasted_iota(jnp.int32, (chunk, chunk), 0)
    col = lax.broadcasted_iota(jnp.int32, (chunk, chunk), 1)
    tri = (col <= row).astype(BF16)
    carry = jnp.zeros((1, LANES), F32)
    for ci in range(s // chunk):
        x = ff_ref[ci * chunk:(ci + 1) * chunk, :] + bf_ref[...]
        lf = jnp.minimum(x, 0.0) - jnp.log1p(jnp.exp(-jnp.abs(x)))
        hi = lf.astype(BF16)
        r1 = lf - hi.astype(F32)
        mid = r1.astype(BF16)
        lo = (r1 - mid.astype(F32)).astype(BF16)
        cs = (jnp.dot(tri, hi, preferred_element_type=F32)
              + jnp.dot(tri, mid, preferred_element_type=F32)
              + jnp.dot(tri, lo, preferred_element_type=F32)) + carry
        carry = cs[chunk - 1:chunk, :]
        o_ref[ci * chunk:(ci + 1) * chunk, :] = -cs


def _fgate(ff, b_f_row, batch, seq, chunk):
    return pl.pallas_call(
        functools.partial(_fgate_kernel, chunk=chunk),
        grid=(batch,),
        in_specs=[pl.BlockSpec((seq, LANES), lambda b: (b, 0)), pl.BlockSpec((1, LANES), lambda b: (0, 0))],
        out_specs=pl.BlockSpec((seq, LANES), lambda b: (b, 0)),
        out_shape=jax.ShapeDtypeStruct((batch * seq, LANES), F32),
        compiler_params=_params("arbitrary"),
        name="fgate",
    )(ff, b_f_row)


def _fill_vt(v_ref, vt_ref, chunk):
    seq = v_ref.shape[0]
    for c in range(seq // chunk):
        blk = v_ref[c * chunk:(c + 1) * chunk, :].astype(F32)
        vt_ref[0:HEAD_DIM, c * chunk:(c + 1) * chunk] = blk.T.astype(BF16)
    row = lax.broadcasted_iota(jnp.int32, (VT_ROWS - HEAD_DIM, seq), 0)
    vt_ref[HEAD_DIM:VT_ROWS, :] = jnp.where(row == 0, 1.0, 0.0).astype(BF16)


def _chain_step(s, m, acc, vt_blk):
    m_new = jnp.maximum(m, jnp.max(s, axis=0, keepdims=True))
    m_safe = jnp.where(m_new == -jnp.inf, 0.0, m_new)
    alpha = jnp.exp2(m - m_safe)
    p = jnp.exp2(s - m_safe).astype(BF16)
    acc_new = alpha * acc + jnp.dot(vt_blk, p, preferred_element_type=F32)
    return m_new, acc_new


def _key_le_query(keys, queries, key0, query0):
    kk = lax.broadcasted_iota(jnp.int32, (keys, queries), 0) + key0
    qq = lax.broadcasted_iota(jnp.int32, (keys, queries), 1) + query0
    return kk <= qq


def _finish(acc):
    o_t = acc[0:HEAD_DIM, :] / acc[HEAD_DIM:HEAD_DIM + 1, :]
    return o_t.T


def _keys_seen(kb, c, tk):
    return min(tk, (c + 1) * QC - kb * tk)


def _causal_pipeline(n_kb, n_chains, per_kb, issue_scores, consume, init):
    state = init
    issue_scores(0, 0, tuple(range(n_chains)))
    for kb in range(n_kb):
        if kb + 1 < n_kb:
            issue_scores((kb + 1) % 2, kb + 1, tuple(range(per_kb * (kb + 1), n_chains)))
        state = consume(kb % 2, kb, tuple(range(per_kb * kb, n_chains)), state)
    return state


def _make_consume(s_ref, vt_ref, tk):
    per_kb = tk // QC

    def consume(slot, kb, chains, state):
        state = list(state)
        for c in chains:
            keys = _keys_seen(kb, c, tk)
            s = s_ref[slot, c, 0:keys, :]
            if c < per_kb * (kb + 1):
                s = jnp.where(_key_le_query(keys, QC, 0, keys - QC), s, -jnp.inf)
            state[c] = _chain_step(s, state[c][0], state[c][1], vt_ref[:, kb * tk:kb * tk + keys])
        return tuple(state)
    return consume


def _init_state(n_chains):
    return tuple((jnp.full((1, QC), -jnp.inf, F32), jnp.zeros((VT_ROWS, QC), F32)) for _ in range(n_chains))


def _fox_kernel(q_ref, k_ref, v_ref, f_ref, w32_ref, o_ref, w16_ref, vt_ref, bcol_ref, s_ref, *, tk):
    h = pl.program_id(1)
    w16_ref[...] = w32_ref[...].astype(BF16)
    seq = k_ref.shape[0]
    nq = seq // QC
    _fill_vt(v_ref, vt_ref, tk)
    lane = lax.broadcasted_iota(jnp.int32, (seq, LANES), 1)
    col = jnp.sum(jnp.where(lane == h, f_ref[...], 0.0), axis=1, keepdims=True)
    bcol_ref[...] = jnp.broadcast_to(col * LOG2E, (seq, LANES))
    q_cs = [q_ref[c * QC:(c + 1) * QC, :].astype(F32).T.astype(BF16) for c in range(nq)]

    def issue_scores(slot, kb, chains):
        for c in chains:
            keys = _keys_seen(kb, c, tk)
            b = bcol_ref[kb * tk:kb * tk + keys, :]
            s = jnp.dot(k_ref[kb * tk:kb * tk + keys, :], q_cs[c], preferred_element_type=F32)
            for u in range(QC // LANES):
                s_ref[slot, c, 0:keys, u * LANES:(u + 1) * LANES] = s[:, u * LANES:(u + 1) * LANES] + b

    state = _causal_pipeline(seq // tk, nq, tk // QC, issue_scores, _make_consume(s_ref, vt_ref, tk),
                             _init_state(nq))
    for c in range(nq):
        o_ref[c * QC:(c + 1) * QC, :] = _finish(state[c][1]).astype(o_ref.dtype)


def _diff_kernel(q_ref, k_ref, v_ref, lam_ref, g_ref, w32_ref, o_ref, w16_ref, vt_ref, s_ref, om_ref,
                 *, tk, out_scale):
    w16_ref[...] = w32_ref[...].astype(BF16)
    seq = k_ref.shape[0]
    nq = seq // QC
    _fill_vt(v_ref, vt_ref, tk)
    q_ts = [q_ref[c * QC:(c + 1) * QC, :].astype(F32).T for c in range(nq)]
    first = _first_map_features(lax.broadcasted_iota(jnp.int32, (HEAD_DIM, QC), 0))
    consume = _make_consume(s_ref, vt_ref, tk)

    def one_map(r, carry):
        keep = first == (r == 0)
        q_cs = [jnp.where(keep, q, 0.0).astype(BF16) for q in q_ts]

        def issue_scores(slot, kb, chains):
            for c in chains:
                keys = _keys_seen(kb, c, tk)
                s_ref[slot, c, 0:keys, :] = jnp.dot(k_ref[kb * tk:kb * tk + keys, :], q_cs[c],
                                                    preferred_element_type=F32)

        state = _causal_pipeline(seq // tk, nq, tk // QC, issue_scores, consume, _init_state(nq))
        for c in range(nq):
            om_ref[r, c * QC:(c + 1) * QC, :] = _finish(state[c][1])
        return carry

    lax.fori_loop(0, 2, one_map, 0)
    o = om_ref[0] - lam_ref[...] * om_ref[1]
    o = o * lax.rsqrt(jnp.mean(o * o, axis=-1, keepdims=True) + RMS_EPS)
    o_ref[...] = (o * g_ref[...] * out_scale).astype(o_ref.dtype)


def _head_block(seq, col0):
    return pl.BlockSpec((seq, HEAD_DIM), lambda b, h: (b, col0 + h))


def _ride_along(w, layer, batch, n_heads):
    rows = w.shape[1] // (batch * n_heads)
    assert rows * batch * n_heads == w.shape[1] and rows % BF16_SUBLANES == 0
    spec_in = pl.BlockSpec((None, rows, w.shape[2]), lambda b, h: (layer, b * n_heads + h, 0))
    spec_out = pl.BlockSpec((rows, w.shape[2]), lambda b, h: (b * n_heads + h, 0))
    return spec_in, spec_out, jax.ShapeDtypeStruct(w.shape[1:], BF16)


def _fox_attention(proj, fcum, w32, layer, batch, seq, n_heads, tk):
    w_in_spec, w_out_spec, w_shape = _ride_along(w32, layer, batch, n_heads)
    return pl.pallas_call(
        functools.partial(_fox_kernel, tk=tk),
        grid=(batch, n_heads),
        in_specs=[_head_block(seq, 0), _head_block(seq, n_heads), _head_block(seq, 2 * n_heads),
                  pl.BlockSpec((seq, LANES), lambda b, h: (b, 0)), w_in_spec],
        out_specs=[_head_block(seq, 0), w_out_spec],
        out_shape=[jax.ShapeDtypeStruct((batch * seq, n_heads * HEAD_DIM), BF16), w_shape],
        scratch_shapes=[pltpu.VMEM((VT_ROWS, seq), BF16), pltpu.VMEM((seq, LANES), F32),
                        pltpu.VMEM((2, seq // QC, tk, QC), F32)],
        compiler_params=_params("arbitrary", "arbitrary"),
        name="fox_attention",
    )(proj, proj, proj, fcum, w32)


def _diff_attention(proj, lam_row, g_row, w32, layer, batch, seq, n_heads, tk, out_scale):
    row = pl.BlockSpec((1, LANES), lambda b, h: (0, 0))
    w_in_spec, w_out_spec, w_shape = _ride_along(w32, layer, batch, n_heads)
    return pl.pallas_call(
        functools.partial(_diff_kernel, tk=tk, out_scale=out_scale),
        grid=(batch, n_heads),
        in_specs=[_head_block(seq, 3 * n_heads), _head_block(seq, 4 * n_heads), _head_block(seq, 5 * n_heads),
                  row, row, w_in_spec],
        out_specs=[_head_block(seq, 0), w_out_spec],
        out_shape=[jax.ShapeDtypeStruct((batch * seq, n_heads * HEAD_DIM), BF16), w_shape],
        scratch_shapes=[pltpu.VMEM((VT_ROWS, seq), BF16), pltpu.VMEM((2, seq // QC, tk, QC), F32),
                        pltpu.VMEM((2, seq, HEAD_DIM), F32)],
        compiler_params=_params("arbitrary", "arbitrary"),
        name="diff_attention",
    )(proj, proj, proj, lam_row, g_row, w32)


ROW_GROUPS = 2


def _outproj_kernel(fo_ref, do_ref, wa_ref, wb_ref, x_ref, gate_ref, lg_ref, lb_ref, o_ref, *, alpha):
    rows = o_ref.shape[0] // ROW_GROUPS
    for g in range(ROW_GROUPS):
        sl = slice(g * rows, (g + 1) * rows)
        y = (jnp.dot(fo_ref[sl, :], wa_ref[...], preferred_element_type=F32)
             + jnp.dot(do_ref[sl, :], wb_ref[...], preferred_element_type=F32))
        z = alpha * x_ref[sl, :] + (1.0 + gate_ref[...]) * y
        o_ref[sl, :] = _layernorm(z, lg_ref[...], lb_ref[...])


def _out_proj(fox_o, diff_o, w_o, layer, x2, mod, gate_base, ln_g, ln_b, tm, rows_per_batch, alpha):
    m, d = x2.shape
    w = fox_o.shape[1]
    bpr = rows_per_batch // tm
    row = pl.BlockSpec((1, d), lambda i: (0, 0))
    return pl.pallas_call(
        functools.partial(_outproj_kernel, alpha=alpha),
        grid=(m // tm,),
        in_specs=[
            pl.BlockSpec((tm, w), lambda i: (i, 0)),
            pl.BlockSpec((tm, w), lambda i: (i, 0)),
            pl.BlockSpec((None, w, d), lambda i: (layer, 0, 0)),
            pl.BlockSpec((None, w, d), lambda i: (layer, 1, 0)),
            pl.BlockSpec((tm, d), lambda i: (i, 0)),
            pl.BlockSpec((None, 1, d), lambda i: (gate_base + i // bpr, 0, 0)),
            row, row,
        ],
        out_specs=pl.BlockSpec((tm, d), lambda i: (i, 0)),
        out_shape=jax.ShapeDtypeStruct((m, d), F32),
        compiler_params=_params("arbitrary"),
        name="out_proj_ln",
    )(fox_o, diff_o, w_o, w_o, x2, mod, ln_g, ln_b)


def _mlp_kernel(x_ref, sc_ref, sh_ref, gate_ref, wu_hbm, wd_hbm, lg_ref, lb_ref, o_ref, wu_buf, wd_buf, sem,
                *, alpha, tf):
    i = pl.program_id(0)
    nf = wu_hbm.shape[1] // tf
    assert nf % 2 == 0

    def copies(f, slot):
        return (pltpu.make_async_copy(wu_hbm.at[:, pl.ds(f * tf, tf)], wu_buf.at[slot], sem.at[0, slot]),
                pltpu.make_async_copy(wd_hbm.at[pl.ds(f * tf, tf), :], wd_buf.at[slot], sem.at[1, slot]))

    def start(f, slot):
        for cp in copies(f, slot):
            cp.start()

    def wait(f, slot):
        for cp in copies(f, slot):
            cp.wait()

    def hidden(h, slot):
        u = jnp.dot(h, wu_buf[slot], preferred_element_type=F32)
        a = jnp.square(jnp.maximum(u, 0.0)).astype(BF16)
        return jnp.dot(a, wd_buf[slot], preferred_element_type=F32)

    @pl.when(i == 0)
    def _():
        start(0, 0)

    h = (x_ref[...] * (1.0 + sc_ref[...]) + sh_ref[...]).astype(BF16)
    for f in range(nf - 1):
        slot = f % 2
        start(f + 1, 1 - slot)
        wait(f, slot)
        if f == 0:
            o_ref[...] = hidden(h, slot)
        else:
            o_ref[...] += hidden(h, slot)

    @pl.when(i + 1 < pl.num_programs(0))
    def _():
        start(0, 0)

    slot = (nf - 1) % 2
    wait(nf - 1, slot)
    rows = o_ref.shape[0] // ROW_GROUPS
    for g in range(ROW_GROUPS):
        sl = slice(g * rows, (g + 1) * rows)
        y = o_ref[sl, :] + hidden(h[sl, :], slot)
        z = alpha * x_ref[sl, :] + (1.0 + gate_ref[...]) * y
        o_ref[sl, :] = _layernorm(z, lg_ref[...], lb_ref[...])


def _mlp(x2, mod, mod_base, batch, w_up, w_down, ln_g, ln_b, tm, tf, rows_per_batch, alpha):
    m, d = x2.shape
    bpr = rows_per_batch // tm

    def mod_spec(part):
        return pl.BlockSpec((None, 1, d), lambda i: (mod_base + part * batch + i // bpr, 0, 0))

    row = pl.BlockSpec((1, d), lambda i: (0, 0))
    return pl.pallas_call(
        functools.partial(_mlp_kernel, alpha=alpha, tf=tf),
        grid=(m // tm,),
        in_specs=[
            pl.BlockSpec((tm, d), lambda i: (i, 0)),
            mod_spec(4), mod_spec(3), mod_spec(5),
            pl.BlockSpec(memory_space=pl.ANY), pl.BlockSpec(memory_space=pl.ANY),
            row, row,
        ],
        out_specs=pl.BlockSpec((tm, d), lambda i: (i, 0)),
        out_shape=jax.ShapeDtypeStruct((m, d), F32),
        scratch_shapes=[pltpu.VMEM((2, d, tf), BF16), pltpu.VMEM((2, tf, d), BF16), pltpu.SemaphoreType.DMA((2, 2))],
        compiler_params=_params("arbitrary"),
        name="mlp_ln",
    )(x2, mod, mod, mod, w_up, w_down, ln_g, ln_b)


def kernel(x, c, positions, w_ada, b_ada, w_in, b_f, lambda_q1, lambda_k1, lambda_q2, lambda_k2,
           subln_g, w_o, ln1_g, ln1_b, w_up, w_down, ln2_g, ln2_b):
    batch, seq, d = x.shape
    depth = w_ada.shape[0]
    m = batch * seq
    width = d // 2
    n_heads = width // HEAD_DIM
    assert n_heads <= LANES and d % (2 * HEAD_DIM) == 0
    alpha = (2.0 * depth) ** 0.25

    tk_attn = 2 * QC
    assert seq % tk_attn == 0
    tm_proj = min(1024, seq)
    tm_out = min(512, seq)
    tm_mlp = min(512, seq)
    tf_mlp = min(1024, w_up.shape[2])
    tn_ada = min(1024, 6 * d)
    cum_chunk = min(256, seq)

    c_pad = jnp.pad(c, ((0, -batch % SUBLANES), (0, 0)))
    mod = _ada_mod(c_pad, w_ada, b_ada, tn_ada)[:, :batch]
    mod = mod.reshape(depth, batch, 6, d).transpose(0, 2, 1, 3).reshape(depth * 6 * batch, 1, d)

    lam = _lam_values(lambda_q1, lambda_k1, lambda_q2, lambda_k2)
    tables = _rope_tables(positions, tm_proj)

    w_main, w_f = _w_in_prep(w_in, width, n_heads, min(256, d))
    w_o16 = w_o.astype(BF16)

    x2 = x.reshape(m, d)
    for l in range(depth):
        base = l * 6 * batch
        b_f_row = jnp.pad(b_f[l], (0, LANES - n_heads))[None, :]

        proj, ff = _in_proj(x2, mod, base, w_main, w_f, l, tables, tm_proj, seq)
        fcum = _fgate(ff, b_f_row, batch, seq, cum_chunk)
        fox_o, w_up16 = _fox_attention(proj, fcum, w_up, l, batch, seq, n_heads, tk_attn)
        diff_o, w_down16 = _diff_attention(proj, lam[l][None, :], subln_g[l][None, :], w_down, l, batch, seq,
                                           n_heads, tk_attn, 1.0 - _lambda_init(l))
        x2 = _out_proj(fox_o, diff_o, w_o16, l, x2, mod, base + 2 * batch,
                       ln1_g[l][None, :], ln1_b[l][None, :], tm_out, seq, alpha)
        x2 = _mlp(x2, mod, base, batch, w_up16, w_down16,
                  ln2_g[l][None, :], ln2_b[l][None, :], tm_mlp, tf_mlp, seq, alpha)
    return x2.reshape(batch, seq, d)
```

```python
import functools
import math

import jax
import jax.numpy as jnp
from jax import lax
from jax.experimental import pallas as pl
from jax.experimental.pallas import tpu as pltpu

HEAD_DIM = 128
DIFF_QK_DIM = HEAD_DIM // 2
ROT_DIM = DIFF_QK_DIM // 4
ROT_HALF = ROT_DIM // 2
ROPE_THETA = 500000.0
LN_EPS = 1e-5
RMS_EPS = 1e-5
LANES = 128
SUBLANES = 8
BF16_SUBLANES = 16
QC = 256
VT_ROWS = HEAD_DIM + BF16_SUBLANES
LOG2E = math.log2(math.e)
VMEM_LIMIT_BYTES = 56 * 1024 * 1024

F32 = jnp.float32
BF16 = jnp.bfloat16


def _params(*semantics):
    return pltpu.CompilerParams(dimension_semantics=semantics, vmem_limit_bytes=VMEM_LIMIT_BYTES)


def _lambda_init(layer_idx):
    return 0.8 - 0.6 * math.exp(-0.3 * layer_idx)


def _layernorm(z, g, b):
    mu = jnp.mean(z, axis=-1, keepdims=True)
    zc = z - mu
    var = jnp.mean(zc * zc, axis=-1, keepdims=True)
    return zc * lax.rsqrt(var + LN_EPS) * g + b


def _ada_kernel(c_ref, w_ref, b_ref, o_ref):
    c = c_ref[...]
    c_act = (c * jax.nn.sigmoid(c)).astype(BF16)
    o_ref[...] = jnp.dot(c_act, w_ref[...].astype(BF16), preferred_element_type=F32) + b_ref[...]


def _ada_mod(c_pad, w_ada, b_ada, tn):
    depth, d, n = w_ada.shape
    rows = c_pad.shape[0]
    return pl.pallas_call(
        _ada_kernel,
        grid=(depth, n // tn),
        in_specs=[
            pl.BlockSpec((rows, d), lambda l, j: (0, 0)),
            pl.BlockSpec((None, d, tn), lambda l, j: (l, 0, j)),
            pl.BlockSpec((None, 1, tn), lambda l, j: (l, 0, j)),
        ],
        out_specs=pl.BlockSpec((None, rows, tn), lambda l, j: (l, 0, j)),
        out_shape=jax.ShapeDtypeStruct((depth, rows, n), F32),
        compiler_params=_params("arbitrary", "arbitrary"),
        name="ada_mod",
    )(c_pad, w_ada, b_ada.reshape(depth, 1, n))


def _lam_kernel(q1_ref, k1_ref, q2_ref, k2_ref, init_ref, o_ref):
    s1 = jnp.sum(q1_ref[...] * k1_ref[...], axis=-1, keepdims=True)
    s2 = jnp.sum(q2_ref[...] * k2_ref[...], axis=-1, keepdims=True)
    lam = jnp.exp(s1) - jnp.exp(s2) + init_ref[...]
    o_ref[...] = jnp.broadcast_to(lam, o_ref.shape)


def _lam_values(lq1, lk1, lq2, lk2):
    depth = lq1.shape[0]
    init = jnp.asarray([[_lambda_init(l)] for l in range(depth)], F32)
    return pl.pallas_call(
        _lam_kernel,
        out_shape=jax.ShapeDtypeStruct((depth, LANES), F32),
        name="lam_values",
    )(lq1, lk1, lq2, lk2, init)


def _rope_table_kernel(pos_ref, invf_ref, c_ref, s_ref):
    ang = pos_ref[...].astype(F32) * invf_ref[...]
    lane = lax.broadcasted_iota(jnp.int32, ang.shape, 1)
    sin = jnp.sin(ang)
    c_ref[...] = jnp.cos(ang)
    s_ref[...] = jnp.where(lane < LANES // 2, -sin, sin)


def _rope_tables(positions, tm):
    m = positions.size
    inv_freq = ROPE_THETA ** (-jnp.arange(0, ROT_DIM, 2, dtype=F32) / ROT_DIM)
    lane = jnp.arange(LANES) % (LANES // 2)
    invf_row = jnp.where(lane < ROT_DIM, inv_freq[lane % ROT_HALF], 0.0).astype(F32)[None, :]
    spec = pl.BlockSpec((tm, LANES), lambda i: (i, 0))
    shape = jax.ShapeDtypeStruct((m, LANES), F32)
    return pl.pallas_call(
        _rope_table_kernel,
        grid=(m // tm,),
        in_specs=[pl.BlockSpec((tm, 1), lambda i: (i, 0)), pl.BlockSpec((1, LANES), lambda i: (0, 0))],
        out_specs=[spec, spec],
        out_shape=[shape, shape],
        compiler_params=_params("arbitrary"),
        name="rope_tables",
    )(positions.reshape(m, 1), invf_row)


def _diff_head_feature(lane):
    swap = DIFF_QK_DIM - ROT_HALF
    return jnp.where((lane >= ROT_HALF) & (lane < ROT_DIM), lane + swap,
                     jnp.where((lane >= DIFF_QK_DIM) & (lane < DIFF_QK_DIM + ROT_HALF), lane - swap, lane))


def _w_in_prep_kernel(wt_ref, main_ref, f_ref, *, width, n_gate):
    gate0 = 3 * width
    diff0 = gate0 + n_gate
    heads = width // HEAD_DIM

    def piece(col0):
        return wt_ref[col0:col0 + HEAD_DIM, :].T.astype(BF16)

    for hd in range(3 * heads):
        main_ref[:, hd * HEAD_DIM:(hd + 1) * HEAD_DIM] = piece(hd * HEAD_DIM)
    lane = lax.broadcasted_iota(jnp.int32, f_ref.shape, 1)
    f_ref[...] = jnp.where(lane < n_gate, piece(gate0), jnp.zeros(f_ref.shape, BF16))
    src = lax.broadcasted_iota(jnp.int32, (HEAD_DIM, HEAD_DIM), 0)
    dst = lax.broadcasted_iota(jnp.int32, (HEAD_DIM, HEAD_DIM), 1)
    reorder = jnp.where(src == _diff_head_feature(dst), 1.0, 0.0).astype(BF16)
    for hd in range(3 * heads):
        p = piece(diff0 + hd * HEAD_DIM)
        if hd < 2 * heads:
            p = jnp.dot(p, reorder, preferred_element_type=F32).astype(BF16)
        main_ref[:, gate0 + hd * HEAD_DIM:gate0 + (hd + 1) * HEAD_DIM] = p


def _w_in_prep(w_in, width, n_gate, rows):
    depth, d, cols = w_in.shape
    return pl.pallas_call(
        functools.partial(_w_in_prep_kernel, width=width, n_gate=n_gate),
        grid=(depth, d // rows),
        in_specs=[pl.BlockSpec((None, cols, rows), lambda l, i: (l, 0, i))],
        out_specs=[pl.BlockSpec((None, rows, 6 * width), lambda l, i: (l, i, 0)),
                   pl.BlockSpec((None, rows, LANES), lambda l, i: (l, i, 0))],
        out_shape=[jax.ShapeDtypeStruct((depth, d, 6 * width), BF16), jax.ShapeDtypeStruct((depth, d, LANES), BF16)],
        compiler_params=_params("arbitrary", "arbitrary"),
        name="w_in_prep",
    )(jnp.swapaxes(w_in, 1, 2))


def _first_map_features(idx):
    return (idx < ROT_HALF) | ((idx >= ROT_DIM) & (idx < DIFF_QK_DIM + ROT_HALF))


def _inproj_kernel(x_ref, sc_ref, sh_ref, w_ref, wf_ref, c_ref, s_ref, proj_ref, ff_ref, h_ref,
                   *, fox_scale, diff_scale):
    j = pl.program_id(1)

    def plain(h, scale):
        acc = jnp.dot(h, w_ref[...], preferred_element_type=F32)
        proj_ref[...] = (acc if scale is None else acc * scale).astype(proj_ref.dtype)

    @pl.when(j == 0)
    def _():
        h = (x_ref[...] * (1.0 + sc_ref[...]) + sh_ref[...]).astype(BF16)
        h_ref[...] = h
        ff_ref[...] = jnp.dot(h, wf_ref[...], preferred_element_type=F32)
        plain(h, fox_scale)

    @pl.when((j == 1) | (j == 2) | (j == 5))
    def _():
        plain(h_ref[...], None)

    @pl.when((j == 3) | (j == 4))
    def _():
        scale = jnp.where(j == 3, diff_scale, 1.0).astype(F32)
        c_eff = c_ref[...] * scale
        s_eff = s_ref[...] * scale
        acc = jnp.dot(h_ref[...], w_ref[...], preferred_element_type=F32)
        for hd in range(acc.shape[1] // LANES):
            xs = acc[:, hd * LANES:(hd + 1) * LANES]
            partner = pltpu.roll(xs, LANES // 2, 1)
            proj_ref[:, hd * LANES:(hd + 1) * LANES] = (xs * c_eff + partner * s_eff).astype(proj_ref.dtype)


def _in_proj(x2, mod, mod_base, w_main, w_f, layer, tables, tm, rows_per_batch):
    m, d = x2.shape
    w = w_main.shape[2] // 6
    bpr = rows_per_batch // tm
    batch = m // rows_per_batch

    def mod_spec(part):
        return pl.BlockSpec((None, 1, d), lambda i, j: (mod_base + part * batch + i // bpr, 0, 0))

    tab_spec = pl.BlockSpec((tm, LANES), lambda i, j: (i, 0))
    kernel = functools.partial(_inproj_kernel, fox_scale=HEAD_DIM ** -0.5 * LOG2E,
                               diff_scale=DIFF_QK_DIM ** -0.5 * LOG2E)
    return pl.pallas_call(
        kernel,
        grid=(m // tm, 6),
        in_specs=[
            pl.BlockSpec((tm, d), lambda i, j: (i, 0)),
            mod_spec(1), mod_spec(0),
            pl.BlockSpec((None, d, w), lambda i, j: (layer, 0, j)),
            pl.BlockSpec((None, d, LANES), lambda i, j: (layer, 0, 0)),
            tab_spec, tab_spec,
        ],
        out_specs=[
            pl.BlockSpec((tm, w), lambda i, j: (i, j)),
            pl.BlockSpec((tm, LANES), lambda i, j: (i, 0)),
        ],
        out_shape=[jax.ShapeDtypeStruct((m, 6 * w), BF16), jax.ShapeDtypeStruct((m, LANES), F32)],
        scratch_shapes=[pltpu.VMEM((tm, d), BF16)],
        compiler_params=_params("arbitrary", "arbitrary"),
        name="in_proj",
    )(x2, mod, mod, w_main, w_f, *tables)


def _fgate_kernel(ff_ref, bf_ref, o_ref, *, chunk):
    s = ff_ref.shape[0]
    row = lax.broadcasted_iota(jnp.int32, (chunk, chunk), 0)
    col = lax.broadcasted_iota(jnp.int32, (chunk, chunk), 1)
    tri = (col <= row).astype(BF16)
    carry = jnp.zeros((1, LANES), F32)
    for ci in range(s // chunk):
        x = ff_ref[ci * chunk:(ci + 1) * chunk, :] + bf_ref[...]
        lf = jnp.minimum(x, 0.0) - jnp.log1p(jnp.exp(-jnp.abs(x)))
        hi = lf.astype(BF16)
        r1 = lf - hi.astype(F32)
        mid = r1.astype(BF16)
        lo = (r1 - mid.astype(F32)).astype(BF16)
        cs = (jnp.dot(tri, hi, preferred_element_type=F32)
              + jnp.dot(tri, mid, preferred_element_type=F32)
              + jnp.dot(tri, lo, preferred_element_type=F32)) + carry
        carry = cs[chunk - 1:chunk, :]
        o_ref[ci * chunk:(ci + 1) * chunk, :] = -cs


def _fgate(ff, b_f_row, batch, seq, chunk):
    return pl.pallas_call(
        functools.partial(_fgate_kernel, chunk=chunk),
        grid=(batch,),
        in_specs=[pl.BlockSpec((seq, LANES), lambda b: (b, 0)), pl.BlockSpec((1, LANES), lambda b: (0, 0))],
        out_specs=pl.BlockSpec((seq, LANES), lambda b: (b, 0)),
        out_shape=jax.ShapeDtypeStruct((batch * seq, LANES), F32),
        compiler_params=_params("arbitrary"),
        name="fgate",
    )(ff, b_f_row)


def _fill_vt(v_ref, vt_ref, chunk):
    seq = v_ref.shape[0]
    for c in range(seq // chunk):
        blk = v_ref[c * chunk:(c + 1) * chunk, :].astype(F32)
        vt_ref[0:HEAD_DIM, c * chunk:(c + 1) * chunk] = blk.T.astype(BF16)
    row = lax.broadcasted_iota(jnp.int32, (VT_ROWS - HEAD_DIM, seq), 0)
    vt_ref[HEAD_DIM:VT_ROWS, :] = jnp.where(row == 0, 1.0, 0.0).astype(BF16)


def _chain_step(s, m, acc, vt_blk):
    m_new = jnp.maximum(m, jnp.max(s, axis=0, keepdims=True))
    m_safe = jnp.where(m_new == -jnp.inf, 0.0, m_new)
    alpha = jnp.exp2(m - m_safe)
    p = jnp.exp2(s - m_safe).astype(BF16)
    acc_new = alpha * acc + jnp.dot(vt_blk, p, preferred_element_type=F32)
    return m_new, acc_new


def _key_le_query(keys, queries, key0, query0):
    kk = lax.broadcasted_iota(jnp.int32, (keys, queries), 0) + key0
    qq = lax.broadcasted_iota(jnp.int32, (keys, queries), 1) + query0
    return kk <= qq


def _finish(acc):
    o_t = acc[0:HEAD_DIM, :] / acc[HEAD_DIM:HEAD_DIM + 1, :]
    return o_t.T


def _keys_seen(kb, c, tk):
    return min(tk, (c + 1) * QC - kb * tk)


def _causal_pipeline(n_kb, n_chains, per_kb, issue_scores, consume, init):
    state = init
    issue_scores(0, 0, tuple(range(n_chains)))
    for kb in range(n_kb):
        if kb + 1 < n_kb:
            issue_scores((kb + 1) % 2, kb + 1, tuple(range(per_kb * (kb + 1), n_chains)))
        state = consume(kb % 2, kb, tuple(range(per_kb * kb, n_chains)), state)
    return state


def _make_consume(s_ref, vt_ref, tk):
    per_kb = tk // QC

    def consume(slot, kb, chains, state):
        state = list(state)
        for c in chains:
            keys = _keys_seen(kb, c, tk)
            s = s_ref[slot, c, 0:keys, :]
            if c < per_kb * (kb + 1):
                s = jnp.where(_key_le_query(keys, QC, 0, keys - QC), s, -jnp.inf)
            state[c] = _chain_step(s, state[c][0], state[c][1], vt_ref[:, kb * tk:kb * tk + keys])
        return tuple(state)
    return consume


def _init_state(n_chains):
    return tuple((jnp.full((1, QC), -jnp.inf, F32), jnp.zeros((VT_ROWS, QC), F32)) for _ in range(n_chains))


def _fox_kernel(q_ref, k_ref, v_ref, f_ref, w32_ref, o_ref, w16_ref, vt_ref, bcol_ref, s_ref, *, tk):
    h = pl.program_id(1)
    w16_ref[...] = w32_ref[...].astype(BF16)
    seq = k_ref.shape[0]
    nq = seq // QC
    _fill_vt(v_ref, vt_ref, tk)
    lane = lax.broadcasted_iota(jnp.int32, (seq, LANES), 1)
    col = jnp.sum(jnp.where(lane == h, f_ref[...], 0.0), axis=1, keepdims=True)
    bcol_ref[...] = jnp.broadcast_to(col * LOG2E, (seq, LANES))
    q_cs = [q_ref[c * QC:(c + 1) * QC, :].astype(F32).T.astype(BF16) for c in range(nq)]

    def issue_scores(slot, kb, chains):
        for c in chains:
            keys = _keys_seen(kb, c, tk)
            b = bcol_ref[kb * tk:kb * tk + keys, :]
            s = jnp.dot(k_ref[kb * tk:kb * tk + keys, :], q_cs[c], preferred_element_type=F32)
            for u in range(QC // LANES):
                s_ref[slot, c, 0:keys, u * LANES:(u + 1) * LANES] = s[:, u * LANES:(u + 1) * LANES] + b

    state = _causal_pipeline(seq // tk, nq, tk // QC, issue_scores, _make_consume(s_ref, vt_ref, tk),
                             _init_state(nq))
    for c in range(nq):
        o_ref[c * QC:(c + 1) * QC, :] = _finish(state[c][1]).astype(o_ref.dtype)


def _diff_kernel(q_ref, k_ref, v_ref, lam_ref, g_ref, w32_ref, o_ref, w16_ref, vt_ref, s_ref, om_ref,
                 *, tk, out_scale):
    w16_ref[...] = w32_ref[...].astype(BF16)
    seq = k_ref.shape[0]
    nq = seq // QC
    _fill_vt(v_ref, vt_ref, tk)
    q_ts = [q_ref[c * QC:(c + 1) * QC, :].astype(F32).T for c in range(nq)]
    first = _first_map_features(lax.broadcasted_iota(jnp.int32, (HEAD_DIM, QC), 0))
    consume = _make_consume(s_ref, vt_ref, tk)

    def one_map(r, carry):
        keep = first == (r == 0)
        q_cs = [jnp.where(keep, q, 0.0).astype(BF16) for q in q_ts]

        def issue_scores(slot, kb, chains):
            for c in chains:
                keys = _keys_seen(kb, c, tk)
                s_ref[slot, c, 0:keys, :] = jnp.dot(k_ref[kb * tk:kb * tk + keys, :], q_cs[c],
                                                    preferred_element_type=F32)

        state = _causal_pipeline(seq // tk, nq, tk // QC, issue_scores, consume, _init_state(nq))
        for c in range(nq):
            om_ref[r, c * QC:(c + 1) * QC, :] = _finish(state[c][1])
        return carry

    lax.fori_loop(0, 2, one_map, 0)
    o = om_ref[0] - lam_ref[...] * om_ref[1]
    o = o * lax.rsqrt(jnp.mean(o * o, axis=-1, keepdims=True) + RMS_EPS)
    o_ref[...] = (o * g_ref[...] * out_scale).astype(o_ref.dtype)


def _head_block(seq, col0):
    return pl.BlockSpec((seq, HEAD_DIM), lambda b, h: (b, col0 + h))


def _ride_along(w, layer, batch, n_heads):
    rows = w.shape[1] // (batch * n_heads)
    assert rows * batch * n_heads == w.shape[1] and rows % BF16_SUBLANES == 0
    spec_in = pl.BlockSpec((None, rows, w.shape[2]), lambda b, h: (layer, b * n_heads + h, 0))
    spec_out = pl.BlockSpec((rows, w.shape[2]), lambda b, h: (b * n_heads + h, 0))
    return spec_in, spec_out, jax.ShapeDtypeStruct(w.shape[1:], BF16)


def _fox_attention(proj, fcum, w32, layer, batch, seq, n_heads, tk):
    w_in_spec, w_out_spec, w_shape = _ride_along(w32, layer, batch, n_heads)
    return pl.pallas_call(
        functools.partial(_fox_kernel, tk=tk),
        grid=(batch, n_heads),
        in_specs=[_head_block(seq, 0), _head_block(seq, n_heads), _head_block(seq, 2 * n_heads),
                  pl.BlockSpec((seq, LANES), lambda b, h: (b, 0)), w_in_spec],
        out_specs=[_head_block(seq, 0), w_out_spec],
        out_shape=[jax.ShapeDtypeStruct((batch * seq, n_heads * HEAD_DIM), BF16), w_shape],
        scratch_shapes=[pltpu.VMEM((VT_ROWS, seq), BF16), pltpu.VMEM((seq, LANES), F32),
                        pltpu.VMEM((2, seq // QC, tk, QC), F32)],
        compiler_params=_params("arbitrary", "arbitrary"),
        name="fox_attention",
    )(proj, proj, proj, fcum, w32)


def _diff_attention(proj, lam_row, g_row, w32, layer, batch, seq, n_heads, tk, out_scale):
    row = pl.BlockSpec((1, LANES), lambda b, h: (0, 0))
    w_in_spec, w_out_spec, w_shape = _ride_along(w32, layer, batch, n_heads)
    return pl.pallas_call(
        functools.partial(_diff_kernel, tk=tk, out_scale=out_scale),
        grid=(batch, n_heads),
        in_specs=[_head_block(seq, 3 * n_heads), _head_block(seq, 4 * n_heads), _head_block(seq, 5 * n_heads),
                  row, row, w_in_spec],
        out_specs=[_head_block(seq, 0), w_out_spec],
        out_shape=[jax.ShapeDtypeStruct((batch * seq, n_heads * HEAD_DIM), BF16), w_shape],
        scratch_shapes=[pltpu.VMEM((VT_ROWS, seq), BF16), pltpu.VMEM((2, seq // QC, tk, QC), F32),
                        pltpu.VMEM((2, seq, HEAD_DIM), F32)],
        compiler_params=_params("arbitrary", "arbitrary"),
        name="diff_attention",
    )(proj, proj, proj, lam_row, g_row, w32)


ROW_GROUPS = 2


def _outproj_kernel(fo_ref, do_ref, wa_ref, wb_ref, x_ref, gate_ref, lg_ref, lb_ref, o_ref, *, alpha):
    rows = o_ref.shape[0] // ROW_GROUPS
    for g in range(ROW_GROUPS):
        sl = slice(g * rows, (g + 1) * rows)
        y = (jnp.dot(fo_ref[sl, :], wa_ref[...], preferred_element_type=F32)
             + jnp.dot(do_ref[sl, :], wb_ref[...], preferred_element_type=F32))
        z = alpha * x_ref[sl, :] + (1.0 + gate_ref[...]) * y
        o_ref[sl, :] = _layernorm(z, lg_ref[...], lb_ref[...])


def _out_proj(fox_o, diff_o, w_o, layer, x2, mod, gate_base, ln_g, ln_b, tm, rows_per_batch, alpha):
    m, d = x2.shape
    w = fox_o.shape[1]
    bpr = rows_per_batch // tm
    row = pl.BlockSpec((1, d), lambda i: (0, 0))
    return pl.pallas_call(
        functools.partial(_outproj_kernel, alpha=alpha),
        grid=(m // tm,),
        in_specs=[
            pl.BlockSpec((tm, w), lambda i: (i, 0)),
            pl.BlockSpec((tm, w), lambda i: (i, 0)),
            pl.BlockSpec((None, w, d), lambda i: (layer, 0, 0)),
            pl.BlockSpec((None, w, d), lambda i: (layer, 1, 0)),
            pl.BlockSpec((tm, d), lambda i: (i, 0)),
            pl.BlockSpec((None, 1, d), lambda i: (gate_base + i // bpr, 0, 0)),
            row, row,
        ],
        out_specs=pl.BlockSpec((tm, d), lambda i: (i, 0)),
        out_shape=jax.ShapeDtypeStruct((m, d), F32),
        compiler_params=_params("arbitrary"),
        name="out_proj_ln",
    )(fox_o, diff_o, w_o, w_o, x2, mod, ln_g, ln_b)


def _mlp_kernel(x_ref, sc_ref, sh_ref, gate_ref, wu_ref, wd_ref, lg_ref, lb_ref, o_ref, h_ref, *, alpha):
    f = pl.program_id(1)
    last = pl.num_programs(1) - 1

    def hidden(h):
        u = jnp.dot(h, wu_ref[...], preferred_element_type=F32)
        a = jnp.square(jnp.maximum(u, 0.0)).astype(BF16)
        return jnp.dot(a, wd_ref[...], preferred_element_type=F32)

    @pl.when(f == 0)
    def _():
        h = (x_ref[...] * (1.0 + sc_ref[...]) + sh_ref[...]).astype(BF16)
        h_ref[...] = h
        o_ref[...] = hidden(h)

    @pl.when((f > 0) & (f < last))
    def _():
        o_ref[...] += hidden(h_ref[...])

    @pl.when(f == last)
    def _():
        rows = o_ref.shape[0] // ROW_GROUPS
        for g in range(ROW_GROUPS):
            sl = slice(g * rows, (g + 1) * rows)
            y = o_ref[sl, :] + hidden(h_ref[sl, :])
            z = alpha * x_ref[sl, :] + (1.0 + gate_ref[...]) * y
            o_ref[sl, :] = _layernorm(z, lg_ref[...], lb_ref[...])


def _mlp(x2, mod, mod_base, batch, w_up, w_down, ln_g, ln_b, tm, tf, rows_per_batch, alpha):
    m, d = x2.shape
    dff = w_up.shape[1]
    assert dff // tf >= 2
    bpr = rows_per_batch // tm

    def mod_spec(part):
        return pl.BlockSpec((None, 1, d), lambda i, f: (mod_base + part * batch + i // bpr, 0, 0))

    row = pl.BlockSpec((1, d), lambda i, f: (0, 0))
    return pl.pallas_call(
        functools.partial(_mlp_kernel, alpha=alpha),
        grid=(m // tm, dff // tf),
        in_specs=[
            pl.BlockSpec((tm, d), lambda i, f: (i, 0)),
            mod_spec(4), mod_spec(3), mod_spec(5),
            pl.BlockSpec((d, tf), lambda i, f: (0, f)),
            pl.BlockSpec((tf, d), lambda i, f: (f, 0)),
            row, row,
        ],
        out_specs=pl.BlockSpec((tm, d), lambda i, f: (i, 0)),
        out_shape=jax.ShapeDtypeStruct((m, d), F32),
        scratch_shapes=[pltpu.VMEM((tm, d), BF16)],
        compiler_params=_params("arbitrary", "arbitrary"),
        name="mlp_ln",
    )(x2, mod, mod, mod, w_up, w_down, ln_g, ln_b)


def kernel(x, c, positions, w_ada, b_ada, w_in, b_f, lambda_q1, lambda_k1, lambda_q2, lambda_k2,
           subln_g, w_o, ln1_g, ln1_b, w_up, w_down, ln2_g, ln2_b):
    batch, seq, d = x.shape
    depth = w_ada.shape[0]
    m = batch * seq
    width = d // 2
    n_heads = width // HEAD_DIM
    assert n_heads <= LANES and d % (2 * HEAD_DIM) == 0
    alpha = (2.0 * depth) ** 0.25

    tk_attn = 2 * QC
    assert seq % tk_attn == 0
    tm_proj = min(1024, seq)
    tm_out = min(512, seq)
    tm_mlp = min(512, seq)
    tf_mlp = min(1024, w_up.shape[2])
    tn_ada = min(1024, 6 * d)
    cum_chunk = min(256, seq)

    c_pad = jnp.pad(c, ((0, -batch % SUBLANES), (0, 0)))
    mod = _ada_mod(c_pad, w_ada, b_ada, tn_ada)[:, :batch]
    mod = mod.reshape(depth, batch, 6, d).transpose(0, 2, 1, 3).reshape(depth * 6 * batch, 1, d)

    lam = _lam_values(lambda_q1, lambda_k1, lambda_q2, lambda_k2)
    tables = _rope_tables(positions, tm_proj)

    w_main, w_f = _w_in_prep(w_in, width, n_heads, min(256, d))
    w_o16 = w_o.astype(BF16)

    x2 = x.reshape(m, d)
    for l in range(depth):
        base = l * 6 * batch
        b_f_row = jnp.pad(b_f[l], (0, LANES - n_heads))[None, :]

        proj, ff = _in_proj(x2, mod, base, w_main, w_f, l, tables, tm_proj, seq)
        fcum = _fgate(ff, b_f_row, batch, seq, cum_chunk)
        fox_o, w_up16 = _fox_attention(proj, fcum, w_up, l, batch, seq, n_heads, tk_attn)
        diff_o, w_down16 = _diff_attention(proj, lam[l][None, :], subln_g[l][None, :], w_down, l, batch, seq,
                                           n_heads, tk_attn, 1.0 - _lambda_init(l))
        x2 = _out_proj(fox_o, diff_o, w_o16, l, x2, mod, base + 2 * batch,
                       ln1_g[l][None, :], ln1_b[l][None, :], tm_out, seq, alpha)
        x2 = _mlp(x2, mod, base, batch, w_up16, w_down16,
                  ln2_g[l][None, :], ln2_b[l][None, :], tm_mlp, tf_mlp, seq, alpha)
    return x2.reshape(batch, seq, d)
```

```python
import functools
import math

import jax
import jax.numpy as jnp
from jax import lax
from jax.experimental import pallas as pl
from jax.experimental.pallas import tpu as pltpu

HEAD_DIM = 128
DIFF_QK_DIM = HEAD_DIM // 2
ROT_DIM = DIFF_QK_DIM // 4
ROT_HALF = ROT_DIM // 2
ROPE_THETA = 500000.0
LN_EPS = 1e-5
RMS_EPS = 1e-5
LANES = 128
SUBLANES = 8
BF16_SUBLANES = 16
QC = 256
VT_ROWS = HEAD_DIM + BF16_SUBLANES
LOG2E = math.log2(math.e)
VMEM_LIMIT_BYTES = 56 * 1024 * 1024

F32 = jnp.float32
BF16 = jnp.bfloat16


def _params(*semantics):
    return pltpu.CompilerParams(dimension_semantics=semantics, vmem_limit_bytes=VMEM_LIMIT_BYTES)


def _lambda_init(layer_idx):
    return 0.8 - 0.6 * math.exp(-0.3 * layer_idx)


def _layernorm(z, g, b):
    mu = jnp.mean(z, axis=-1, keepdims=True)
    zc = z - mu
    var = jnp.mean(zc * zc, axis=-1, keepdims=True)
    return zc * lax.rsqrt(var + LN_EPS) * g + b


def _ada_kernel(c_ref, w_ref, b_ref, o_ref):
    c = c_ref[...]
    c_act = (c * jax.nn.sigmoid(c)).astype(BF16)
    o_ref[...] = jnp.dot(c_act, w_ref[...].astype(BF16), preferred_element_type=F32) + b_ref[...]


def _ada_mod(c_pad, w_ada, b_ada, tn):
    depth, d, n = w_ada.shape
    rows = c_pad.shape[0]
    return pl.pallas_call(
        _ada_kernel,
        grid=(depth, n // tn),
        in_specs=[
            pl.BlockSpec((rows, d), lambda l, j: (0, 0)),
            pl.BlockSpec((None, d, tn), lambda l, j: (l, 0, j)),
            pl.BlockSpec((None, 1, tn), lambda l, j: (l, 0, j)),
        ],
        out_specs=pl.BlockSpec((None, rows, tn), lambda l, j: (l, 0, j)),
        out_shape=jax.ShapeDtypeStruct((depth, rows, n), F32),
        compiler_params=_params("arbitrary", "arbitrary"),
        name="ada_mod",
    )(c_pad, w_ada, b_ada.reshape(depth, 1, n))


def _lam_kernel(q1_ref, k1_ref, q2_ref, k2_ref, init_ref, o_ref):
    s1 = jnp.sum(q1_ref[...] * k1_ref[...], axis=-1, keepdims=True)
    s2 = jnp.sum(q2_ref[...] * k2_ref[...], axis=-1, keepdims=True)
    lam = jnp.exp(s1) - jnp.exp(s2) + init_ref[...]
    o_ref[...] = jnp.broadcast_to(lam, o_ref.shape)


def _lam_values(lq1, lk1, lq2, lk2):
    depth = lq1.shape[0]
    init = jnp.asarray([[_lambda_init(l)] for l in range(depth)], F32)
    return pl.pallas_call(
        _lam_kernel,
        out_shape=jax.ShapeDtypeStruct((depth, LANES), F32),
        name="lam_values",
    )(lq1, lk1, lq2, lk2, init)


def _rope_table_kernel(pos_ref, invf_ref, c_ref, s_ref):
    ang = pos_ref[...].astype(F32) * invf_ref[...]
    lane = lax.broadcasted_iota(jnp.int32, ang.shape, 1)
    sin = jnp.sin(ang)
    c_ref[...] = jnp.cos(ang)
    s_ref[...] = jnp.where(lane < LANES // 2, -sin, sin)


def _rope_tables(positions, tm):
    m = positions.size
    inv_freq = ROPE_THETA ** (-jnp.arange(0, ROT_DIM, 2, dtype=F32) / ROT_DIM)
    lane = jnp.arange(LANES) % (LANES // 2)
    invf_row = jnp.where(lane < ROT_DIM, inv_freq[lane % ROT_HALF], 0.0).astype(F32)[None, :]
    spec = pl.BlockSpec((tm, LANES), lambda i: (i, 0))
    shape = jax.ShapeDtypeStruct((m, LANES), F32)
    return pl.pallas_call(
        _rope_table_kernel,
        grid=(m // tm,),
        in_specs=[pl.BlockSpec((tm, 1), lambda i: (i, 0)), pl.BlockSpec((1, LANES), lambda i: (0, 0))],
        out_specs=[spec, spec],
        out_shape=[shape, shape],
        compiler_params=_params("arbitrary"),
        name="rope_tables",
    )(positions.reshape(m, 1), invf_row)


def _diff_head_feature(lane):
    swap = DIFF_QK_DIM - ROT_HALF
    return jnp.where((lane >= ROT_HALF) & (lane < ROT_DIM), lane + swap,
                     jnp.where((lane >= DIFF_QK_DIM) & (lane < DIFF_QK_DIM + ROT_HALF), lane - swap, lane))


def _w_in_prep_kernel(wt_ref, main_ref, f_ref, *, width, n_gate):
    gate0 = 3 * width
    diff0 = gate0 + n_gate
    heads = width // HEAD_DIM

    def piece(col0):
        return wt_ref[col0:col0 + HEAD_DIM, :].T.astype(BF16)

    for hd in range(3 * heads):
        main_ref[:, hd * HEAD_DIM:(hd + 1) * HEAD_DIM] = piece(hd * HEAD_DIM)
    lane = lax.broadcasted_iota(jnp.int32, f_ref.shape, 1)
    f_ref[...] = jnp.where(lane < n_gate, piece(gate0), jnp.zeros(f_ref.shape, BF16))
    src = lax.broadcasted_iota(jnp.int32, (HEAD_DIM, HEAD_DIM), 0)
    dst = lax.broadcasted_iota(jnp.int32, (HEAD_DIM, HEAD_DIM), 1)
    reorder = jnp.where(src == _diff_head_feature(dst), 1.0, 0.0).astype(BF16)
    for hd in range(3 * heads):
        p = piece(diff0 + hd * HEAD_DIM)
        if hd < 2 * heads:
            p = jnp.dot(p, reorder, preferred_element_type=F32).astype(BF16)
        main_ref[:, gate0 + hd * HEAD_DIM:gate0 + (hd + 1) * HEAD_DIM] = p


def _w_in_prep(w_in, width, n_gate, rows):
    depth, d, cols = w_in.shape
    return pl.pallas_call(
        functools.partial(_w_in_prep_kernel, width=width, n_gate=n_gate),
        grid=(depth, d // rows),
        in_specs=[pl.BlockSpec((None, cols, rows), lambda l, i: (l, 0, i))],
        out_specs=[pl.BlockSpec((None, rows, 6 * width), lambda l, i: (l, i, 0)),
                   pl.BlockSpec((None, rows, LANES), lambda l, i: (l, i, 0))],
        out_shape=[jax.ShapeDtypeStruct((depth, d, 6 * width), BF16), jax.ShapeDtypeStruct((depth, d, LANES), BF16)],
        compiler_params=_params("arbitrary", "arbitrary"),
        name="w_in_prep",
    )(jnp.swapaxes(w_in, 1, 2))


def _first_map_features(idx):
    return (idx < ROT_HALF) | ((idx >= ROT_DIM) & (idx < DIFF_QK_DIM + ROT_HALF))


def _inproj_kernel(x_ref, sc_ref, sh_ref, w_ref, wf_ref, c_ref, s_ref, wo32_ref, proj_ref, ff_ref, wo16_ref, h_ref,
                   *, fox_scale, diff_scale):
    j = pl.program_id(1)

    def plain(h, scale):
        acc = jnp.dot(h, w_ref[...], preferred_element_type=F32)
        proj_ref[...] = (acc if scale is None else acc * scale).astype(proj_ref.dtype)

    @pl.when(j == 0)
    def _():
        h = (x_ref[...] * (1.0 + sc_ref[...]) + sh_ref[...]).astype(BF16)
        h_ref[...] = h
        ff_ref[...] = jnp.dot(h, wf_ref[...], preferred_element_type=F32)
        wo16_ref[...] = wo32_ref[...].astype(BF16)
        plain(h, fox_scale)

    @pl.when((j == 1) | (j == 2) | (j == 5))
    def _():
        plain(h_ref[...], None)

    @pl.when((j == 3) | (j == 4))
    def _():
        scale = jnp.where(j == 3, diff_scale, 1.0).astype(F32)
        c_eff = c_ref[...] * scale
        s_eff = s_ref[...] * scale
        acc = jnp.dot(h_ref[...], w_ref[...], preferred_element_type=F32)
        for hd in range(acc.shape[1] // LANES):
            xs = acc[:, hd * LANES:(hd + 1) * LANES]
            partner = pltpu.roll(xs, LANES // 2, 1)
            proj_ref[:, hd * LANES:(hd + 1) * LANES] = (xs * c_eff + partner * s_eff).astype(proj_ref.dtype)


def _in_proj(x2, mod, mod_base, w_main, w_f, w_o, layer, tables, tm, rows_per_batch):
    m, d = x2.shape
    w = w_main.shape[2] // 6
    wo_rows = w_o.shape[1] // (m // tm)
    assert wo_rows * (m // tm) == w_o.shape[1] and wo_rows % BF16_SUBLANES == 0
    bpr = rows_per_batch // tm
    batch = m // rows_per_batch

    def mod_spec(part):
        return pl.BlockSpec((None, 1, d), lambda i, j: (mod_base + part * batch + i // bpr, 0, 0))

    tab_spec = pl.BlockSpec((tm, LANES), lambda i, j: (i, 0))
    kernel = functools.partial(_inproj_kernel, fox_scale=HEAD_DIM ** -0.5 * LOG2E,
                               diff_scale=DIFF_QK_DIM ** -0.5 * LOG2E)
    return pl.pallas_call(
        kernel,
        grid=(m // tm, 6),
        in_specs=[
            pl.BlockSpec((tm, d), lambda i, j: (i, 0)),
            mod_spec(1), mod_spec(0),
            pl.BlockSpec((None, d, w), lambda i, j: (layer, 0, j)),
            pl.BlockSpec((None, d, LANES), lambda i, j: (layer, 0, 0)),
            tab_spec, tab_spec,
            pl.BlockSpec((None, wo_rows, w_o.shape[2]), lambda i, j: (layer, i, 0)),
        ],
        out_specs=[
            pl.BlockSpec((tm, w), lambda i, j: (i, j)),
            pl.BlockSpec((tm, LANES), lambda i, j: (i, 0)),
            pl.BlockSpec((wo_rows, w_o.shape[2]), lambda i, j: (i, 0)),
        ],
        out_shape=[jax.ShapeDtypeStruct((m, 6 * w), BF16), jax.ShapeDtypeStruct((m, LANES), F32),
                   jax.ShapeDtypeStruct(w_o.shape[1:], BF16)],
        scratch_shapes=[pltpu.VMEM((tm, d), BF16)],
        compiler_params=_params("arbitrary", "arbitrary"),
        name="in_proj",
    )(x2, mod, mod, w_main, w_f, *tables, w_o)


def _fgate_kernel(ff_ref, bf_ref, o_ref, *, chunk):
    s = ff_ref.shape[0]
    row = lax.broadcasted_iota(jnp.int32, (chunk, chunk), 0)
    col = lax.broadcasted_iota(jnp.int32, (chunk, chunk), 1)
    tri = (col <= row).astype(BF16)
    carry = jnp.zeros((1, LANES), F32)
    for ci in range(s // chunk):
        x = ff_ref[ci * chunk:(ci + 1) * chunk, :] + bf_ref[...]
        lf = jnp.minimum(x, 0.0) - jnp.log1p(jnp.exp(-jnp.abs(x)))
        hi = lf.astype(BF16)
        r1 = lf - hi.astype(F32)
        mid = r1.astype(BF16)
        lo = (r1 - mid.astype(F32)).astype(BF16)
        cs = (jnp.dot(tri, hi, preferred_element_type=F32)
              + jnp.dot(tri, mid, preferred_element_type=F32)
              + jnp.dot(tri, lo, preferred_element_type=F32)) + carry
        carry = cs[chunk - 1:chunk, :]
        o_ref[ci * chunk:(ci + 1) * chunk, :] = -cs


def _fgate(ff, b_f_row, batch, seq, chunk):
    return pl.pallas_call(
        functools.partial(_fgate_kernel, chunk=chunk),
        grid=(batch,),
        in_specs=[pl.BlockSpec((seq, LANES), lambda b: (b, 0)), pl.BlockSpec((1, LANES), lambda b: (0, 0))],
        out_specs=pl.BlockSpec((seq, LANES), lambda b: (b, 0)),
        out_shape=jax.ShapeDtypeStruct((batch * seq, LANES), F32),
        compiler_params=_params("arbitrary"),
        name="fgate",
    )(ff, b_f_row)


def _fill_vt(v_ref, vt_ref, chunk):
    seq = v_ref.shape[0]
    for c in range(seq // chunk):
        blk = v_ref[c * chunk:(c + 1) * chunk, :].astype(F32)
        vt_ref[0:HEAD_DIM, c * chunk:(c + 1) * chunk] = blk.T.astype(BF16)
    row = lax.broadcasted_iota(jnp.int32, (VT_ROWS - HEAD_DIM, seq), 0)
    vt_ref[HEAD_DIM:VT_ROWS, :] = jnp.where(row == 0, 1.0, 0.0).astype(BF16)


def _chain_step(s, m, acc, vt_blk):
    m_new = jnp.maximum(m, jnp.max(s, axis=0, keepdims=True))
    m_safe = jnp.where(m_new == -jnp.inf, 0.0, m_new)
    alpha = jnp.exp2(m - m_safe)
    p = jnp.exp2(s - m_safe).astype(BF16)
    acc_new = alpha * acc + jnp.dot(vt_blk, p, preferred_element_type=F32)
    return m_new, acc_new


def _key_le_query(keys, queries, key0, query0):
    kk = lax.broadcasted_iota(jnp.int32, (keys, queries), 0) + key0
    qq = lax.broadcasted_iota(jnp.int32, (keys, queries), 1) + query0
    return kk <= qq


def _finish(acc):
    o_t = acc[0:HEAD_DIM, :] / acc[HEAD_DIM:HEAD_DIM + 1, :]
    return o_t.T


def _keys_seen(kb, c, tk):
    return min(tk, (c + 1) * QC - kb * tk)


def _causal_pipeline(n_kb, n_chains, per_kb, issue_scores, consume, init):
    state = init
    issue_scores(0, 0, tuple(range(n_chains)))
    for kb in range(n_kb):
        if kb + 1 < n_kb:
            issue_scores((kb + 1) % 2, kb + 1, tuple(range(per_kb * (kb + 1), n_chains)))
        state = consume(kb % 2, kb, tuple(range(per_kb * kb, n_chains)), state)
    return state


def _make_consume(s_ref, vt_ref, tk):
    per_kb = tk // QC

    def consume(slot, kb, chains, state):
        state = list(state)
        for c in chains:
            keys = _keys_seen(kb, c, tk)
            s = s_ref[slot, c, 0:keys, :]
            if c < per_kb * (kb + 1):
                s = jnp.where(_key_le_query(keys, QC, 0, keys - QC), s, -jnp.inf)
            state[c] = _chain_step(s, state[c][0], state[c][1], vt_ref[:, kb * tk:kb * tk + keys])
        return tuple(state)
    return consume


def _init_state(n_chains):
    return tuple((jnp.full((1, QC), -jnp.inf, F32), jnp.zeros((VT_ROWS, QC), F32)) for _ in range(n_chains))


def _fox_kernel(q_ref, k_ref, v_ref, f_ref, w32_ref, o_ref, w16_ref, vt_ref, bcol_ref, s_ref, *, tk):
    h = pl.program_id(1)
    w16_ref[...] = w32_ref[...].astype(BF16)
    seq = k_ref.shape[0]
    nq = seq // QC
    _fill_vt(v_ref, vt_ref, tk)
    lane = lax.broadcasted_iota(jnp.int32, (seq, LANES), 1)
    col = jnp.sum(jnp.where(lane == h, f_ref[...], 0.0), axis=1, keepdims=True)
    bcol_ref[...] = jnp.broadcast_to(col * LOG2E, (seq, LANES))
    q_cs = [q_ref[c * QC:(c + 1) * QC, :].astype(F32).T.astype(BF16) for c in range(nq)]

    def issue_scores(slot, kb, chains):
        for c in chains:
            keys = _keys_seen(kb, c, tk)
            b = bcol_ref[kb * tk:kb * tk + keys, :]
            s = jnp.dot(k_ref[kb * tk:kb * tk + keys, :], q_cs[c], preferred_element_type=F32)
            for u in range(QC // LANES):
                s_ref[slot, c, 0:keys, u * LANES:(u + 1) * LANES] = s[:, u * LANES:(u + 1) * LANES] + b

    state = _causal_pipeline(seq // tk, nq, tk // QC, issue_scores, _make_consume(s_ref, vt_ref, tk),
                             _init_state(nq))
    for c in range(nq):
        o_ref[c * QC:(c + 1) * QC, :] = _finish(state[c][1]).astype(o_ref.dtype)


def _diff_kernel(q_ref, k_ref, v_ref, lam_ref, g_ref, w32_ref, o_ref, w16_ref, vt_ref, s_ref, om_ref,
                 *, tk, out_scale):
    w16_ref[...] = w32_ref[...].astype(BF16)
    seq = k_ref.shape[0]
    nq = seq // QC
    _fill_vt(v_ref, vt_ref, tk)
    q_ts = [q_ref[c * QC:(c + 1) * QC, :].astype(F32).T for c in range(nq)]
    first = _first_map_features(lax.broadcasted_iota(jnp.int32, (HEAD_DIM, QC), 0))
    consume = _make_consume(s_ref, vt_ref, tk)

    def one_map(r, carry):
        keep = first == (r == 0)
        q_cs = [jnp.where(keep, q, 0.0).astype(BF16) for q in q_ts]

        def issue_scores(slot, kb, chains):
            for c in chains:
                keys = _keys_seen(kb, c, tk)
                s_ref[slot, c, 0:keys, :] = jnp.dot(k_ref[kb * tk:kb * tk + keys, :], q_cs[c],
                                                    preferred_element_type=F32)

        state = _causal_pipeline(seq // tk, nq, tk // QC, issue_scores, consume, _init_state(nq))
        for c in range(nq):
            om_ref[r, c * QC:(c + 1) * QC, :] = _finish(state[c][1])
        return carry

    lax.fori_loop(0, 2, one_map, 0)
    o = om_ref[0] - lam_ref[...] * om_ref[1]
    o = o * lax.rsqrt(jnp.mean(o * o, axis=-1, keepdims=True) + RMS_EPS)
    o_ref[...] = (o * g_ref[...] * out_scale).astype(o_ref.dtype)


def _head_block(seq, col0):
    return pl.BlockSpec((seq, HEAD_DIM), lambda b, h: (b, col0 + h))


def _ride_along(w, layer, batch, n_heads):
    rows = w.shape[1] // (batch * n_heads)
    assert rows * batch * n_heads == w.shape[1] and rows % BF16_SUBLANES == 0
    spec_in = pl.BlockSpec((None, rows, w.shape[2]), lambda b, h: (layer, b * n_heads + h, 0))
    spec_out = pl.BlockSpec((rows, w.shape[2]), lambda b, h: (b * n_heads + h, 0))
    return spec_in, spec_out, jax.ShapeDtypeStruct(w.shape[1:], BF16)


def _fox_attention(proj, fcum, w32, layer, batch, seq, n_heads, tk):
    w_in_spec, w_out_spec, w_shape = _ride_along(w32, layer, batch, n_heads)
    return pl.pallas_call(
        functools.partial(_fox_kernel, tk=tk),
        grid=(batch, n_heads),
        in_specs=[_head_block(seq, 0), _head_block(seq, n_heads), _head_block(seq, 2 * n_heads),
                  pl.BlockSpec((seq, LANES), lambda b, h: (b, 0)), w_in_spec],
        out_specs=[_head_block(seq, 0), w_out_spec],
        out_shape=[jax.ShapeDtypeStruct((batch * seq, n_heads * HEAD_DIM), BF16), w_shape],
        scratch_shapes=[pltpu.VMEM((VT_ROWS, seq), BF16), pltpu.VMEM((seq, LANES), F32),
                        pltpu.VMEM((2, seq // QC, tk, QC), F32)],
        compiler_params=_params("arbitrary", "arbitrary"),
        name="fox_attention",
    )(proj, proj, proj, fcum, w32)


def _diff_attention(proj, lam_row, g_row, w32, layer, batch, seq, n_heads, tk, out_scale):
    row = pl.BlockSpec((1, LANES), lambda b, h: (0, 0))
    w_in_spec, w_out_spec, w_shape = _ride_along(w32, layer, batch, n_heads)
    return pl.pallas_call(
        functools.partial(_diff_kernel, tk=tk, out_scale=out_scale),
        grid=(batch, n_heads),
        in_specs=[_head_block(seq, 3 * n_heads), _head_block(seq, 4 * n_heads), _head_block(seq, 5 * n_heads),
                  row, row, w_in_spec],
        out_specs=[_head_block(seq, 0), w_out_spec],
        out_shape=[jax.ShapeDtypeStruct((batch * seq, n_heads * HEAD_DIM), BF16), w_shape],
        scratch_shapes=[pltpu.VMEM((VT_ROWS, seq), BF16), pltpu.VMEM((2, seq // QC, tk, QC), F32),
                        pltpu.VMEM((2, seq, HEAD_DIM), F32)],
        compiler_params=_params("arbitrary", "arbitrary"),
        name="diff_attention",
    )(proj, proj, proj, lam_row, g_row, w32)


ROW_GROUPS = 2


def _outproj_kernel(fo_ref, do_ref, wa_ref, wb_ref, x_ref, gate_ref, lg_ref, lb_ref, o_ref, *, alpha):
    rows = o_ref.shape[0] // ROW_GROUPS
    for g in range(ROW_GROUPS):
        sl = slice(g * rows, (g + 1) * rows)
        y = (jnp.dot(fo_ref[sl, :], wa_ref[...], preferred_element_type=F32)
             + jnp.dot(do_ref[sl, :], wb_ref[...], preferred_element_type=F32))
        z = alpha * x_ref[sl, :] + (1.0 + gate_ref[...]) * y
        o_ref[sl, :] = _layernorm(z, lg_ref[...], lb_ref[...])


def _out_proj(fox_o, diff_o, w_o, x2, mod, gate_base, ln_g, ln_b, tm, rows_per_batch, alpha):
    m, d = x2.shape
    w = fox_o.shape[1]
    bpr = rows_per_batch // tm
    row = pl.BlockSpec((1, d), lambda i: (0, 0))
    return pl.pallas_call(
        functools.partial(_outproj_kernel, alpha=alpha),
        grid=(m // tm,),
        in_specs=[
            pl.BlockSpec((tm, w), lambda i: (i, 0)),
            pl.BlockSpec((tm, w), lambda i: (i, 0)),
            pl.BlockSpec((w, d), lambda i: (0, 0)),
            pl.BlockSpec((w, d), lambda i: (1, 0)),
            pl.BlockSpec((tm, d), lambda i: (i, 0)),
            pl.BlockSpec((None, 1, d), lambda i: (gate_base + i // bpr, 0, 0)),
            row, row,
        ],
        out_specs=pl.BlockSpec((tm, d), lambda i: (i, 0)),
        out_shape=jax.ShapeDtypeStruct((m, d), F32),
        compiler_params=_params("arbitrary"),
        name="out_proj_ln",
    )(fox_o, diff_o, w_o, w_o, x2, mod, ln_g, ln_b)


def _mlp_kernel(x_ref, sc_ref, sh_ref, gate_ref, wu_ref, wd_ref, lg_ref, lb_ref, o_ref, h_ref, *, alpha):
    f = pl.program_id(1)
    last = pl.num_programs(1) - 1

    def hidden(h):
        u = jnp.dot(h, wu_ref[...], preferred_element_type=F32)
        a = jnp.square(jnp.maximum(u, 0.0)).astype(BF16)
        return jnp.dot(a, wd_ref[...], preferred_element_type=F32)

    @pl.when(f == 0)
    def _():
        h = (x_ref[...] * (1.0 + sc_ref[...]) + sh_ref[...]).astype(BF16)
        h_ref[...] = h
        o_ref[...] = hidden(h)

    @pl.when((f > 0) & (f < last))
    def _():
        o_ref[...] += hidden(h_ref[...])

    @pl.when(f == last)
    def _():
        rows = o_ref.shape[0] // ROW_GROUPS
        for g in range(ROW_GROUPS):
            sl = slice(g * rows, (g + 1) * rows)
            y = o_ref[sl, :] + hidden(h_ref[sl, :])
            z = alpha * x_ref[sl, :] + (1.0 + gate_ref[...]) * y
            o_ref[sl, :] = _layernorm(z, lg_ref[...], lb_ref[...])


def _mlp(x2, mod, mod_base, batch, w_up, w_down, ln_g, ln_b, tm, tf, rows_per_batch, alpha):
    m, d = x2.shape
    dff = w_up.shape[1]
    assert dff // tf >= 2
    bpr = rows_per_batch // tm

    def mod_spec(part):
        return pl.BlockSpec((None, 1, d), lambda i, f: (mod_base + part * batch + i // bpr, 0, 0))

    row = pl.BlockSpec((1, d), lambda i, f: (0, 0))
    return pl.pallas_call(
        functools.partial(_mlp_kernel, alpha=alpha),
        grid=(m // tm, dff // tf),
        in_specs=[
            pl.BlockSpec((tm, d), lambda i, f: (i, 0)),
            mod_spec(4), mod_spec(3), mod_spec(5),
            pl.BlockSpec((d, tf), lambda i, f: (0, f)),
            pl.BlockSpec((tf, d), lambda i, f: (f, 0)),
            row, row,
        ],
        out_specs=pl.BlockSpec((tm, d), lambda i, f: (i, 0)),
        out_shape=jax.ShapeDtypeStruct((m, d), F32),
        scratch_shapes=[pltpu.VMEM((tm, d), BF16)],
        compiler_params=_params("arbitrary", "arbitrary"),
        name="mlp_ln",
    )(x2, mod, mod, mod, w_up, w_down, ln_g, ln_b)


def kernel(x, c, positions, w_ada, b_ada, w_in, b_f, lambda_q1, lambda_k1, lambda_q2, lambda_k2,
           subln_g, w_o, ln1_g, ln1_b, w_up, w_down, ln2_g, ln2_b):
    batch, seq, d = x.shape
    depth = w_ada.shape[0]
    m = batch * seq
    width = d // 2
    n_heads = width // HEAD_DIM
    assert n_heads <= LANES and d % (2 * HEAD_DIM) == 0
    alpha = (2.0 * depth) ** 0.25

    tk_attn = 2 * QC
    assert seq % tk_attn == 0
    tm_proj = min(1024, seq)
    tm_out = min(512, seq)
    tm_mlp = min(512, seq)
    tf_mlp = min(1024, w_up.shape[2])
    tn_ada = min(1024, 6 * d)
    cum_chunk = min(256, seq)

    c_pad = jnp.pad(c, ((0, -batch % SUBLANES), (0, 0)))
    mod = _ada_mod(c_pad, w_ada, b_ada, tn_ada)[:, :batch]
    mod = mod.reshape(depth, batch, 6, d).transpose(0, 2, 1, 3).reshape(depth * 6 * batch, 1, d)

    lam = _lam_values(lambda_q1, lambda_k1, lambda_q2, lambda_k2)
    tables = _rope_tables(positions, tm_proj)

    w_main, w_f = _w_in_prep(w_in, width, n_heads, min(256, d))

    x2 = x.reshape(m, d)
    for l in range(depth):
        base = l * 6 * batch
        b_f_row = jnp.pad(b_f[l], (0, LANES - n_heads))[None, :]

        proj, ff, w_o16 = _in_proj(x2, mod, base, w_main, w_f, w_o, l, tables, tm_proj, seq)
        fcum = _fgate(ff, b_f_row, batch, seq, cum_chunk)
        fox_o, w_up16 = _fox_attention(proj, fcum, w_up, l, batch, seq, n_heads, tk_attn)
        diff_o, w_down16 = _diff_attention(proj, lam[l][None, :], subln_g[l][None, :], w_down, l, batch, seq,
                                           n_heads, tk_attn, 1.0 - _lambda_init(l))
        x2 = _out_proj(fox_o, diff_o, w_o16, x2, mod, base + 2 * batch,
                       ln1_g[l][None, :], ln1_b[l][None, :], tm_out, seq, alpha)
        x2 = _mlp(x2, mod, base, batch, w_up16, w_down16,
                  ln2_g[l][None, :], ln2_b[l][None, :], tm_mlp, tf_mlp, seq, alpha)
    return x2.reshape(batch, seq, d)
```
